```python
import math, functools
import jax, jax.numpy as jnp
from jax import lax
import numpy as np

D_MODEL = 1024
BATCH = 8
SEQ = 2048
DEPTH = 1
DEC_BATCH = 32
DEC_SEQ = 4
PAST_LEN = 8192
PAGE_SIZE = 128

A_HEADS = 8
A_HEAD_DIM = 64
A_QK = A_HEADS * 2 * A_HEAD_DIM
A_V = A_HEADS * 2 * A_HEAD_DIM
Q_BLOCK = 128

M_HEADS = 4
M_HEAD_DIM = 256
M_WIDTH = M_HEADS * M_HEAD_DIM
M_CONV = 4
M_CHUNK = 64

P_HEADS = 8
P_NKEYS = 128
P_EXPERTS = P_NKEYS * P_NKEYS
P_KEY_DIM = 256
P_TOPK = 16
P_TOKEN_BLOCK = 256

ALPHA = (2.0 * DEPTH) ** 0.25
BETA = (8.0 * DEPTH) ** -0.25
LN_EPS = 1e-5
PAD_LOG_INPUT_GATE = -1e30

SPLIT_SIZES = (A_QK, A_QK, A_V, M_WIDTH, M_WIDTH, M_WIDTH, M_HEADS, M_HEADS, D_MODEL, D_MODEL)
SPLIT_POINTS = tuple(sum(SPLIT_SIZES[:i + 1]) for i in range(len(SPLIT_SIZES) - 1))
D_IN = sum(SPLIT_SIZES)
F_GATE_OFFSET = SPLIT_POINTS[6]

kernel_name = 'hybrid_diffattn_mlstm_peer_step'


def layer_norm(x, g, b):
    xf = x.astype(jnp.float32)
    mu = jnp.mean(xf, -1, keepdims=True)
    var = jnp.mean(jnp.square(xf - mu), -1, keepdims=True)
    return ((xf - mu) * lax.rsqrt(var + LN_EPS) * g.astype(jnp.float32) + b.astype(jnp.float32)).astype(x.dtype)


def rms_norm(x, g):
    xf = x.astype(jnp.float32)
    r = lax.rsqrt(jnp.mean(jnp.square(xf), -1, keepdims=True) + LN_EPS)
    return (xf * r * g.astype(jnp.float32)).astype(x.dtype)


def head_layer_norm(h, g):
    mu = jnp.mean(h, -1, keepdims=True)
    var = jnp.mean(jnp.square(h - mu), -1, keepdims=True)
    return (h - mu) * lax.rsqrt(var + LN_EPS) * g.astype(jnp.float32).reshape(h.shape[-2:])


def diff_attn_core(q, k, v, q_pos, k_pos, lam):
    s = jnp.einsum('bqhcd,bkhcd->bhcqk', q, k).astype(jnp.float32) * (A_HEAD_DIM ** -0.5)
    mask = k_pos[None, :] <= q_pos[:, None]
    p = jax.nn.softmax(jnp.where(mask, s, -jnp.inf), axis=-1)
    a = p[:, :, 0] - lam * p[:, :, 1]
    return jnp.einsum('bhqk,bkhe->bqhe', a.astype(v.dtype), v)


def attend_prompt(q, k, v, lam):
    B, S = q.shape[0], q.shape[1]
    k_pos = jnp.arange(S)

    def one_block(i):
        qb = lax.dynamic_slice_in_dim(q, i * Q_BLOCK, Q_BLOCK, axis=1)
        q_pos = i * Q_BLOCK + jnp.arange(Q_BLOCK)
        return diff_attn_core(qb, k, v, q_pos, k_pos, lam)

    out = lax.map(one_block, jnp.arange(S // Q_BLOCK))
    return jnp.moveaxis(out, 0, 1).reshape(B, S, A_HEADS, 2 * A_HEAD_DIM)


def attend_sample(q, k, v, lam, past_k, past_v):
    B, S = q.shape[0], q.shape[1]
    P = past_k.shape[1]
    k_all = jnp.concatenate([past_k.reshape(B, P, A_HEADS, 2, A_HEAD_DIM), k], axis=1)
    v_all = jnp.concatenate([past_v, v], axis=1)
    k_pos = jnp.arange(P + S)
    q_pos = P + jnp.arange(S)
    return diff_attn_core(q, k_all, v_all, q_pos, k_pos, lam)


def mlstm_chunkwise(q, k, v, logi, logf, C0, n0, m0):
    B, H, S, dh = q.shape
    L = min(M_CHUNK, S)
    pad = (-S) % L
    if pad:
        pw = ((0, 0), (0, 0), (0, pad), (0, 0))
        q, k, v = jnp.pad(q, pw), jnp.pad(k, pw), jnp.pad(v, pw)
        logi = jnp.pad(logi, pw[:3], constant_values=PAD_LOG_INPUT_GATE)
        logf = jnp.pad(logf, pw[:3], constant_values=0.0)
    nc = (S + pad) // L

    def to_chunks(a):
        return jnp.moveaxis(a.reshape((B, H, nc, L) + a.shape[3:]), 2, 0)

    tril = jnp.tril(jnp.ones((L, L), dtype=bool))

    def step(carry, inp):
        C, n, m = carry
        qc, kc, vc, ic, fc = inp
        b = jnp.cumsum(fc, axis=-1)
        D = jnp.where(tril, b[..., :, None] - b[..., None, :] + ic[..., None, :], -jnp.inf)
        m_t = jnp.maximum(b + m[..., None], jnp.max(D, axis=-1))
        W = jnp.exp(D - m_t[..., None])
        inter = jnp.exp(b + m[..., None] - m_t)
        sw = jnp.einsum('bhtd,bhsd->bhts', qc, kc) * W
        num = inter[..., None] * jnp.einsum('bhtd,bhed->bhte', qc, C) + jnp.einsum('bhts,bhse->bhte', sw, vc)
        dot = inter * jnp.einsum('bhtd,bhd->bht', qc, n) + jnp.sum(sw, axis=-1)
        h = num / jnp.maximum(jnp.abs(dot), jnp.exp(-m_t))[..., None]
        m_new = m_t[..., -1]
        decay = jnp.exp(b[..., -1] + m - m_new)
        w = jnp.exp(b[..., -1:] - b + ic - m_new[..., None])
        C_new = decay[..., None, None] * C + jnp.einsum('bhs,bhse,bhsd->bhed', w, vc, kc)
        n_new = decay[..., None] * n + jnp.einsum('bhs,bhsd->bhd', w, kc)
        return (C_new, n_new, m_new), h

    (C, n, m), hs = lax.scan(step, (C0, n0, m0), (to_chunks(q), to_chunks(k), to_chunks(v), to_chunks(logi), to_chunks(logf)))
    h = jnp.moveaxis(hs, 0, 2).reshape(B, H, nc * L, dh)[:, :, :S]
    return h, C, n, m


def token_mix(x, attend, C0, n0, m0, conv0, lam_init, w_in, b_in, lam_q1, lam_k1, lam_q2, lam_k2, subln_g,
              w_conv, b_conv, w_qm, w_km, mnorm_g, w_a, w_b, w_o):
    f32 = jnp.float32
    B, S, _ = x.shape
    proj = jnp.einsum('bsd,de->bse', x, w_in) + b_in
    qa, ka, va, u, vm, om, ig, fg, ga, gb = jnp.split(proj, SPLIT_POINTS, axis=-1)

    qa = qa.reshape(B, S, A_HEADS, 2, A_HEAD_DIM)
    k_rows = ka.reshape(B, S, A_HEADS, 2 * A_HEAD_DIM)
    v_rows = va.reshape(B, S, A_HEADS, 2 * A_HEAD_DIM)
    lam = (jnp.exp(jnp.sum(lam_q1.astype(f32) * lam_k1.astype(f32)))
           - jnp.exp(jnp.sum(lam_q2.astype(f32) * lam_k2.astype(f32))) + lam_init)
    oa = attend(qa, k_rows.reshape(B, S, A_HEADS, 2, A_HEAD_DIM), v_rows, lam)
    oa = rms_norm(oa, subln_g) * (1.0 - lam_init)
    ya = jnp.einsum('bse,ed->bsd', oa.reshape(B, S, A_V), w_a)

    u_ext = jnp.concatenate([conv0.astype(u.dtype), u], axis=1)
    uc = b_conv
    for j in range(M_CONV):
        uc = uc + u_ext[:, j:j + S] * w_conv[j]
    uc = jax.nn.silu(uc).reshape(B, S, M_HEADS, M_HEAD_DIM)
    new_conv = u_ext[:, u_ext.shape[1] - (M_CONV - 1):]
    qm = jnp.einsum('bshd,hde->bhse', uc, w_qm).astype(f32)
    km = (jnp.einsum('bshd,hde->bhse', uc, w_km) * (M_HEAD_DIM ** -0.5)).astype(f32)
    vmh = jnp.swapaxes(vm.reshape(B, S, M_HEADS, M_HEAD_DIM), 1, 2).astype(f32)
    logi = jnp.swapaxes(ig.astype(f32), 1, 2)
    logf = jax.nn.log_sigmoid(jnp.swapaxes(fg.astype(f32), 1, 2))
    h, C, n, m = mlstm_chunkwise(qm, km, vmh, logi, logf, C0.astype(f32), n0.astype(f32), m0.astype(f32))
    h = head_layer_norm(jnp.swapaxes(h, 1, 2), mnorm_g).reshape(B, S, M_WIDTH)
    yb = jnp.einsum('bse,ed->bsd', (jax.nn.sigmoid(om.astype(f32)) * h).astype(x.dtype), w_b)

    merged = jax.nn.sigmoid(ga) * ya + jax.nn.sigmoid(gb) * yb
    out = jnp.einsum('bse,ed->bsd', merged, w_o)
    return out, (k_rows, v_rows, C, n, m, new_conv)


def peer_ffn(x, w_pq, p_keys, p_u, p_v):
    B, S, D = x.shape
    T = B * S
    blk = min(P_TOKEN_BLOCK, T)
    pad = (-T) % blk
    xt = jnp.pad(x.reshape(T, D), ((0, pad), (0, 0)))

    def one_block(xb):
        q = jnp.einsum('td,de->te', xb, w_pq).reshape(blk, P_HEADS, 2, P_KEY_DIM // 2)
        s = jnp.einsum('thcd,hcnd->thcn', q, p_keys).astype(jnp.float32)
        sv, si = lax.top_k(s, P_TOPK)
        cand = (sv[:, :, 0, :, None] + sv[:, :, 1, None, :]).reshape(blk, P_HEADS, P_TOPK * P_TOPK)
        cidx = (si[:, :, 0, :, None] * P_NKEYS + si[:, :, 1, None, :]).reshape(blk, P_HEADS, P_TOPK * P_TOPK)
        top, pos = lax.top_k(cand, P_TOPK)
        eidx = jnp.take_along_axis(cidx, pos, axis=-1)
        g = jax.nn.softmax(top, axis=-1)
        ue = p_u[eidx]
        ve = p_v[eidx]
        a = jax.nn.gelu(jnp.einsum('td,thkd->thk', xb, ue).astype(jnp.float32), approximate=False)
        return jnp.einsum('thk,thkd->td', (g * a).astype(xb.dtype), ve)

    out = lax.map(one_block, xt.reshape(-1, blk, D))
    return out.reshape(-1, D)[:T].reshape(B, S, D)


def decoder_layer(x, attend, state, lam_init, mix_w, ln1_g, ln1_b, peer_w, ln2_g, ln2_b):
    C0, n0, m0, conv0 = state
    mix, new_state = token_mix(x, attend, C0, n0, m0, conv0, lam_init, *mix_w)
    h1 = layer_norm(ALPHA * x + mix, ln1_g, ln1_b)
    y = layer_norm(ALPHA * h1 + peer_ffn(h1, *peer_w), ln2_g, ln2_b)
    return y, new_state


def setup_inputs(seed: int = 0) -> dict:
    key = jax.random.key(seed)
    ks = list(jax.random.split(key, 40))
    f32 = jnp.float32

    def nrm(i, shape, scale):
        return jax.random.normal(ks[i], shape, f32) * scale

    n_pages = PAST_LEN // PAGE_SIZE
    n_pool = (DEC_BATCH * n_pages * 5) // 4
    page_table = jax.random.permutation(ks[0], n_pool)[:DEC_BATCH * n_pages].reshape(DEC_BATCH, n_pages).astype(jnp.int32)
    kv_shape = (DEPTH, n_pool, PAGE_SIZE, A_HEADS, 2 * A_HEAD_DIM)
    b_in = nrm(13, (DEPTH, D_IN), 0.02).at[:, F_GATE_OFFSET:F_GATE_OFFSET + M_HEADS].add(jnp.linspace(3.0, 6.0, M_HEADS))
    return {
        'x_prompt': nrm(1, (BATCH, SEQ, D_MODEL), 1.0),
        'x_sample': nrm(2, (DEC_BATCH, DEC_SEQ, D_MODEL), 1.0),
        'cache_k': nrm(3, kv_shape, 1.0),
        'cache_v': nrm(4, kv_shape, 1.0),
        'state_C': nrm(5, (DEPTH, DEC_BATCH, M_HEADS, M_HEAD_DIM, M_HEAD_DIM), 0.1),
        'state_n': nrm(6, (DEPTH, DEC_BATCH, M_HEADS, M_HEAD_DIM), 0.5),
        'state_m': nrm(7, (DEPTH, DEC_BATCH, M_HEADS), 0.5),
        'state_conv': nrm(8, (DEPTH, DEC_BATCH, M_CONV - 1, M_WIDTH), 1.0),
        'page_table': page_table,
        'w_in': nrm(9, (DEPTH, D_MODEL, D_IN), D_MODEL ** -0.5),
        'b_in': b_in,
        'lam_q1': nrm(10, (DEPTH, A_HEAD_DIM), 0.1),
        'lam_k1': nrm(11, (DEPTH, A_HEAD_DIM), 0.1),
        'lam_q2': nrm(12, (DEPTH, A_HEAD_DIM), 0.1),
        'lam_k2': nrm(14, (DEPTH, A_HEAD_DIM), 0.1),
        'subln_g': 1.0 + nrm(15, (DEPTH, 2 * A_HEAD_DIM), 0.02),
        'w_conv': nrm(16, (DEPTH, M_CONV, M_WIDTH), M_CONV ** -0.5),
        'b_conv': nrm(17, (DEPTH, M_WIDTH), 0.02),
        'w_qm': nrm(18, (DEPTH, M_HEADS, M_HEAD_DIM, M_HEAD_DIM), M_HEAD_DIM ** -0.5),
        'w_km': nrm(19, (DEPTH, M_HEADS, M_HEAD_DIM, M_HEAD_DIM), M_HEAD_DIM ** -0.5),
        'mnorm_g': 1.0 + nrm(20, (DEPTH, M_WIDTH), 0.02),
        'w_a': nrm(21, (DEPTH, A_V, D_MODEL), A_V ** -0.5),
        'w_b': nrm(22, (DEPTH, M_WIDTH, D_MODEL), M_WIDTH ** -0.5),
        'w_o': nrm(23, (DEPTH, D_MODEL, D_MODEL), BETA * D_MODEL ** -0.5),
        'ln1_g': 1.0 + nrm(24, (DEPTH, D_MODEL), 0.02),
        'ln1_b': nrm(25, (DEPTH, D_MODEL), 0.02),
        'w_pq': nrm(26, (DEPTH, D_MODEL, P_HEADS * P_KEY_DIM), D_MODEL ** -0.5),
        'p_keys': nrm(27, (DEPTH, P_HEADS, 2, P_NKEYS, P_KEY_DIM // 2), (P_KEY_DIM // 2) ** -0.5),
        'p_u': nrm(28, (DEPTH, P_EXPERTS, D_MODEL), D_MODEL ** -0.5),
        'p_v': nrm(29, (DEPTH, P_EXPERTS, D_MODEL), BETA * P_HEADS ** -0.5),
        'ln2_g': 1.0 + nrm(30, (DEPTH, D_MODEL), 0.02),
        'ln2_b': nrm(31, (DEPTH, D_MODEL), 0.02),
    }


def reference(x_prompt, x_sample, cache_k, cache_v, state_C, state_n, state_m, state_conv, page_table,
              w_in, b_in, lam_q1, lam_k1, lam_q2, lam_k2, subln_g, w_conv, b_conv, w_qm, w_km, mnorm_g,
              w_a, w_b, w_o, ln1_g, ln1_b, w_pq, p_keys, p_u, p_v, ln2_g, ln2_b):
    f32 = jnp.float32
    B = x_prompt.shape[0]
    DB = x_sample.shape[0]
    zero_state = (jnp.zeros((B, M_HEADS, M_HEAD_DIM, M_HEAD_DIM), f32),
                  jnp.zeros((B, M_HEADS, M_HEAD_DIM), f32),
                  jnp.zeros((B, M_HEADS), f32),
                  jnp.zeros((B, M_CONV - 1, M_WIDTH), x_prompt.dtype))
    hp, hs = x_prompt, x_sample
    new_p, new_s = [], []
    for l in range(DEPTH):
        lam_init = 0.8 - 0.6 * math.exp(-0.3 * l)
        mix_w = (w_in[l], b_in[l], lam_q1[l], lam_k1[l], lam_q2[l], lam_k2[l], subln_g[l], w_conv[l], b_conv[l],
                 w_qm[l], w_km[l], mnorm_g[l], w_a[l], w_b[l], w_o[l])
        peer_w = (w_pq[l], p_keys[l], p_u[l], p_v[l])
        past_k = cache_k[l][page_table].reshape(DB, -1, A_HEADS, 2 * A_HEAD_DIM)
        past_v = cache_v[l][page_table].reshape(DB, -1, A_HEADS, 2 * A_HEAD_DIM)
        hp, sp = decoder_layer(hp, attend_prompt, zero_state, lam_init, mix_w, ln1_g[l], ln1_b[l], peer_w, ln2_g[l], ln2_b[l])
        hs, ss = decoder_layer(hs, functools.partial(attend_sample, past_k=past_k, past_v=past_v),
                               (state_C[l], state_n[l], state_m[l], state_conv[l]), lam_init, mix_w,
                               ln1_g[l], ln1_b[l], peer_w, ln2_g[l], ln2_b[l])
        new_p.append(sp)
        new_s.append(ss)
    k_p, v_p, C_p, n_p, m_p, conv_p = [jnp.stack([st[i] for st in new_p]) for i in range(6)]
    k_s, v_s, C_s, n_s, m_s, conv_s = [jnp.stack([st[i] for st in new_s]) for i in range(6)]
    return (hp, hs, k_p, v_p, C_p, n_p, m_p, conv_p, k_s, v_s, C_s, n_s, m_s, conv_s)
```

```python
import functools
import math

import jax
import jax.numpy as jnp
from jax import lax
from jax.experimental import pallas as pl
from jax.experimental.pallas import tpu as pltpu

F32 = jnp.float32
BF16 = jnp.bfloat16

D_MODEL = 1024
A_HEADS = 8
A_HEAD_DIM = 64
A_HD2 = 2 * A_HEAD_DIM
M_HEADS = 4
M_HEAD_DIM = 256
M_CONV = 4
M_CHUNK = 64
P_HEADS = 8
P_NKEYS = 128
P_HALF_DIM = 128
P_TOPK = 16
P_EXPERTS = P_NKEYS * P_NKEYS
PAGE_SIZE = 128
DEPTH = 1
ALPHA = (2.0 * DEPTH) ** 0.25
LN_EPS = 1e-5
PAD_LOG_INPUT_GATE = -1e30
LAM_INIT = 0.8 - 0.6 * math.exp(-0.3 * 0)

N_MAIN_GROUPS = 8
GATE_COLS = 2 * M_HEADS
LANES = 128
VMEM_LIMIT = 56 * 1024 * 1024

NT_DIMS = (((1,), (1,)), ((), ()))


def _cparams(sem):
    return pltpu.CompilerParams(dimension_semantics=sem, vmem_limit_bytes=VMEM_LIMIT)


def _dot(a, b):
    return jnp.dot(a, b, preferred_element_type=F32)


def _dot_nt(a, b):
    return lax.dot_general(a, b, NT_DIMS, preferred_element_type=F32)


def _in_proj_kernel(x_ref, w_ref, b_ref, wg_ref, bg_ref,
                    q_ref, k_ref, kb_ref, v_ref, vb_ref, u_ref, vm_ref, om_ref, ga_ref, gb_ref,
                    g_ref):
    j = pl.program_id(1)
    xb = x_ref[...].astype(BF16)
    acc = _dot(xb, w_ref[...]) + b_ref[...]

    @pl.when(j == 0)
    def _():
        q_ref[...] = (acc * (A_HEAD_DIM ** -0.5)).astype(BF16)
        g_ref[...] = _dot(xb, wg_ref[...]) + bg_ref[...]

    @pl.when(j == 1)
    def _():
        k_ref[...] = acc
        kb_ref[...] = acc.astype(BF16)

    @pl.when(j == 2)
    def _():
        v_ref[...] = acc
        vb_ref[...] = acc.astype(BF16)

    @pl.when(j == 3)
    def _():
        u_ref[...] = acc

    @pl.when(j == 4)
    def _():
        vm_ref[...] = acc.astype(BF16)

    @pl.when(j == 5)
    def _():
        om_ref[...] = acc

    @pl.when(j == 6)
    def _():
        ga_ref[...] = acc

    @pl.when(j == 7)
    def _():
        gb_ref[...] = acc


def _in_proj(x, w_main, b_main, w_gate, b_gate):
    T = x.shape[0]
    tm = min(T, 512)
    assert T % tm == 0
    row = lambda i, j: (i, 0)
    f32o = jax.ShapeDtypeStruct((T, D_MODEL), F32)
    bf16o = jax.ShapeDtypeStruct((T, D_MODEL), BF16)
    ospec = pl.BlockSpec((tm, D_MODEL), row)
    return pl.pallas_call(
        _in_proj_kernel,
        grid=(T // tm, N_MAIN_GROUPS),
        in_specs=[
            pl.BlockSpec((tm, D_MODEL), row),
            pl.BlockSpec((D_MODEL, D_MODEL), lambda i, j: (0, j)),
            pl.BlockSpec((1, D_MODEL), lambda i, j: (0, j)),
            pl.BlockSpec((D_MODEL, LANES), lambda i, j: (0, 0)),
            pl.BlockSpec((1, LANES), lambda i, j: (0, 0)),
        ],
        out_specs=[ospec] * 10 + [pl.BlockSpec((tm, LANES), row)],
        out_shape=[bf16o, f32o, bf16o, f32o, bf16o, f32o, bf16o, f32o, f32o, f32o,
                   jax.ShapeDtypeStruct((T, LANES), F32)],
        compiler_params=_cparams(("arbitrary", "arbitrary")),
        name="in_proj",
    )(x, w_main, b_main, w_gate, b_gate)


def _lam_value(lq1, lk1, lq2, lk2):
    a = jnp.exp(jnp.sum(lq1 * lk1, axis=1, keepdims=True))
    b = jnp.exp(jnp.sum(lq2 * lk2, axis=1, keepdims=True))
    return a - b + LAM_INIT


def _sub_norm(o, g):
    r = lax.rsqrt(jnp.mean(o * o, axis=1, keepdims=True) + LN_EPS)
    return o * r * g * (1.0 - LAM_INIT)


def _attn_prompt_kernel(lq1_ref, lk1_ref, lq2_ref, lk2_ref, g_ref, q_ref, k_ref, v_ref, o_ref, *, tq):
    i = pl.program_id(2)
    q = q_ref[...]
    lane = lax.broadcasted_iota(jnp.int32, q.shape, 1)
    zero = jnp.zeros_like(q)
    qs = jnp.concatenate([jnp.where(lane < A_HEAD_DIM, q, zero),
                          jnp.where(lane >= A_HEAD_DIM, q, zero)], axis=0)

    def step(j, carry, masked):
        m, l, acc = carry
        r0 = pl.multiple_of(j * tq, tq)
        kj = k_ref[pl.ds(r0, tq), :]
        vj = v_ref[pl.ds(r0, tq), :]
        s = _dot_nt(qs, kj)
        if masked:
            row = lax.broadcasted_iota(jnp.int32, s.shape, 0)
            col = lax.broadcasted_iota(jnp.int32, s.shape, 1)
            qpos = jnp.where(row < tq, row, row - tq)
            s = jnp.where(col <= qpos, s, -jnp.inf)
        m_new = jnp.maximum(m, jnp.max(s, axis=1, keepdims=True))
        alpha = jnp.exp(m - m_new)
        p = jnp.exp(s - m_new)
        l = alpha * l + jnp.sum(p, axis=1, keepdims=True)
        acc = alpha * acc + _dot(p.astype(BF16), vj)
        return m_new, l, acc

    init = (jnp.full((2 * tq, 1), -jnp.inf, F32), jnp.zeros((2 * tq, 1), F32),
            jnp.zeros((2 * tq, A_HD2), F32))
    carry = lax.fori_loop(0, i, lambda j, c: step(j, c, False), init)
    m, l, acc = step(i, carry, True)
    o = acc / l
    lam = _lam_value(lq1_ref[...], lk1_ref[...], lq2_ref[...], lk2_ref[...])
    o = o[:tq] - lam * o[tq:]
    o_ref[...] = _sub_norm(o, g_ref[...]).astype(BF16)


def _attn_prompt(q, kb, vb, lam_vecs, subln_g, B, S):
    tq = min(S, 256)
    nq = S // tq
    vec = pl.BlockSpec((1, A_HEAD_DIM), lambda b, h, i: (0, 0))
    return pl.pallas_call(
        functools.partial(_attn_prompt_kernel, tq=tq),
        grid=(B, A_HEADS, nq),
        in_specs=[vec, vec, vec, vec,
                  pl.BlockSpec((1, A_HD2), lambda b, h, i: (0, 0)),
                  pl.BlockSpec((tq, A_HD2), lambda b, h, i: (b * nq + i, h)),
                  pl.BlockSpec((S, A_HD2), lambda b, h, i: (b, h)),
                  pl.BlockSpec((S, A_HD2), lambda b, h, i: (b, h))],
        out_specs=pl.BlockSpec((tq, A_HD2), lambda b, h, i: (b * nq + i, h)),
        out_shape=jax.ShapeDtypeStruct((B * S, D_MODEL), BF16),
        compiler_params=_cparams(("arbitrary", "arbitrary", "arbitrary")),
        name="attn_prompt",
    )(*lam_vecs, subln_g, q, kb, vb)


SAMPLE_ROWS = 8
PAGES_PER_STEP = 4


def _attn_sample_kernel(pt_ref, lq1_ref, lk1_ref, lq2_ref, lk2_ref, g_ref, q_ref, kn_ref, vn_ref, *rest,
                        n_steps, dec_seq):
    pp = PAGES_PER_STEP
    k_refs, v_refs = rest[:pp], rest[pp:2 * pp]
    o_ref, qbd_ref, m_ref, l_ref, acc_ref = rest[2 * pp:]
    step_id = pl.program_id(1)
    rows = A_HEADS * SAMPLE_ROWS

    @pl.when(step_id == 0)
    def _():
        q8 = q_ref[0]
        qt = jnp.concatenate([q8] * A_HEADS, axis=0)
        row = lax.broadcasted_iota(jnp.int32, qt.shape, 0)
        col = lax.broadcasted_iota(jnp.int32, qt.shape, 1)
        same_head = (row // SAMPLE_ROWS) == (col // A_HD2)
        first_map = (row % SAMPLE_ROWS) < dec_seq
        first_half = (col % A_HD2) < A_HEAD_DIM
        keep = same_head & (first_map == first_half)
        qbd_ref[...] = jnp.where(keep, qt, jnp.zeros_like(qt))
        m_ref[...] = jnp.full(m_ref.shape, -jnp.inf, F32)
        l_ref[...] = jnp.zeros(l_ref.shape, F32)
        acc_ref[...] = jnp.zeros(acc_ref.shape, F32)

    def update(kt, vt, mask):
        s = _dot_nt(qbd_ref[...], kt)
        if mask is not None:
            s = jnp.where(mask, s, -jnp.inf)
        m = m_ref[...]
        m_new = jnp.maximum(m, jnp.max(s, axis=1, keepdims=True))
        alpha = jnp.exp(m - m_new)
        p = jnp.exp(s - m_new)
        l_ref[...] = alpha * l_ref[...] + jnp.sum(p, axis=1, keepdims=True)
        acc_ref[...] = alpha * acc_ref[...] + _dot(p.astype(BF16), vt)
        m_ref[...] = m_new

    for t in range(pp):
        update(k_refs[t][0].astype(BF16), v_refs[t][0].astype(BF16), None)

    @pl.when(step_id == n_steps - 1)
    def _():
        pad = jnp.zeros((PAGE_SIZE - SAMPLE_ROWS, D_MODEL), BF16)
        kt = jnp.concatenate([kn_ref[0].astype(BF16), pad], axis=0)
        vt = jnp.concatenate([vn_ref[0].astype(BF16), pad], axis=0)
        row = lax.broadcasted_iota(jnp.int32, (rows, PAGE_SIZE), 0)
        col = lax.broadcasted_iota(jnp.int32, (rows, PAGE_SIZE), 1)
        update(kt, vt, col <= (row % dec_seq))
        lam = _lam_value(lq1_ref[...], lk1_ref[...], lq2_ref[...], lk2_ref[...])
        g = g_ref[...]
        for h in range(A_HEADS):
            rs = slice(h * SAMPLE_ROWS, (h + 1) * SAMPLE_ROWS)
            cs = slice(h * A_HD2, (h + 1) * A_HD2)
            o = acc_ref[rs, cs] / l_ref[rs, :]
            o = o - lam * pltpu.roll(o, SAMPLE_ROWS - dec_seq, 0)
            o_ref[0, :, cs] = _sub_norm(o, g)


def _attn_sample(q8, kn8, vn8, cache_k, cache_v, page_table, lam_vecs, subln_g):
    DB, n_pages = page_table.shape
    pp = PAGES_PER_STEP
    assert n_pages % pp == 0
    n_steps = n_pages // pp
    dec_seq = SAMPLE_ROWS // 2
    rows = A_HEADS * SAMPLE_ROWS
    vec = pl.BlockSpec((1, A_HEAD_DIM), lambda b, s, pt: (0, 0))
    per_b = pl.BlockSpec((1, SAMPLE_ROWS, D_MODEL), lambda b, s, pt: (b, 0, 0))

    def page_spec(t):
        return pl.BlockSpec((1, PAGE_SIZE, D_MODEL), lambda b, s, pt: (pt[b, s * pp + t], 0, 0))

    grid_spec = pltpu.PrefetchScalarGridSpec(
        num_scalar_prefetch=1,
        grid=(DB, n_steps),
        in_specs=[vec, vec, vec, vec, pl.BlockSpec((1, A_HD2), lambda b, s, pt: (0, 0)),
                  per_b, per_b, per_b]
                 + [page_spec(t) for t in range(pp)] + [page_spec(t) for t in range(pp)],
        out_specs=per_b,
        scratch_shapes=[pltpu.VMEM((rows, D_MODEL), BF16),
                        pltpu.VMEM((rows, 1), F32), pltpu.VMEM((rows, 1), F32),
                        pltpu.VMEM((rows, D_MODEL), F32)],
    )
    return pl.pallas_call(
        functools.partial(_attn_sample_kernel, n_steps=n_steps, dec_seq=dec_seq),
        grid_spec=grid_spec,
        out_shape=jax.ShapeDtypeStruct((DB, SAMPLE_ROWS, D_MODEL), F32),
        compiler_params=_cparams(("arbitrary", "arbitrary")),
        name="attn_sample",
    )(page_table, *lam_vecs, subln_g, q8, kn8, vn8, *([cache_k] * pp), *([cache_v] * pp))


CONV_HALO = 8


def _log_sigmoid(x):
    return -(jnp.maximum(-x, 0.0) + jnp.log1p(jnp.exp(-jnp.abs(x))))


def _mlstm_kernel(u_ref, halo_ref, vm_ref, om_ref, icol_ref, fcol_ref, irow_ref, frow_ref,
                  wconv_ref, bconv_ref, wq_ref, wk_ref, wkt_ref, g_ref, c0_ref, n0_ref, m0_ref,
                  hb_ref, c_ref, n_ref, m_ref,
                  ext_ref, uc_ref, q_ref, k_ref, ct_ref, nrow_ref, *, S, L):
    kscale = M_HEAD_DIM ** -0.5
    ext_ref[0:CONV_HALO, :] = halo_ref[0]
    ext_ref[CONV_HALO:CONV_HALO + S, :] = u_ref[...]
    rb = min(S, 512)
    for r0 in range(0, S, rb):
        acc = jnp.broadcast_to(bconv_ref[...], (rb, M_HEAD_DIM))
        for jj in range(M_CONV):
            off = CONV_HALO - (M_CONV - 1) + jj + r0
            acc = acc + ext_ref[off:off + rb, :] * wconv_ref[jj:jj + 1, :]
        uc = (acc * jax.nn.sigmoid(acc)).astype(BF16)
        uc_ref[r0:r0 + rb, :] = uc
        q_ref[r0:r0 + rb, :] = _dot(uc, wq_ref[0]).astype(BF16)
        k_ref[r0:r0 + rb, :] = (_dot(uc, wk_ref[0]) * kscale).astype(BF16)

    ct_ref[...] = c0_ref[0, 0].T
    nrow_ref[...] = n0_ref[0, 0]
    ti = lax.broadcasted_iota(jnp.int32, (L, L), 0)
    si = lax.broadcasted_iota(jnp.int32, (L, L), 1)
    tril = si <= ti
    gain = g_ref[...]

    def chunk(c, m_prev):
        r0 = pl.multiple_of(c * L, L)
        rows = pl.ds(r0, L)
        qc = q_ref[rows, :]
        kc = k_ref[rows, :]
        vc = vm_ref[rows, :]
        kct = (_dot_nt(wkt_ref[0], uc_ref[rows, :]) * kscale).astype(BF16)
        icol = icol_ref[0, 0, rows, :]
        fcol = _log_sigmoid(fcol_ref[0, 0, rows, :])
        irow = irow_ref[0, 0, pl.ds(c, 1), :]
        frow = _log_sigmoid(frow_ref[0, 0, pl.ds(c, 1), :])
        b_col = jnp.sum(jnp.where(tril, frow, 0.0), axis=1, keepdims=True)
        b_row = jnp.sum(jnp.where(ti <= si, fcol, 0.0), axis=0, keepdims=True)
        dmat = jnp.where(tril, b_col - b_row + irow, -jnp.inf)
        m_t = jnp.maximum(b_col + m_prev, jnp.max(dmat, axis=1, keepdims=True))
        wmat = jnp.exp(dmat - m_t)
        inter = jnp.exp(b_col + m_prev - m_t)
        sw = _dot(qc, kct) * wmat
        ctb = ct_ref[...].astype(BF16)
        num = inter * _dot(qc, ctb) + _dot(sw.astype(BF16), vc)
        qn = jnp.sum(qc.astype(F32) * nrow_ref[...], axis=1, keepdims=True)
        dot = inter * qn + jnp.sum(sw, axis=1, keepdims=True)
        h = num / jnp.maximum(jnp.abs(dot), jnp.exp(-m_t))
        m_new = m_t[L - 1:L, :]
        b_last = b_col[L - 1:L, :]
        decay = jnp.exp(b_last + m_prev - m_new)
        w_col = jnp.exp(b_last - b_col + icol - m_new)
        wv = (w_col * vc.astype(F32)).astype(BF16)
        ct_ref[...] = decay * ct_ref[...] + _dot(kct, wv)
        nrow_ref[...] = decay * nrow_ref[...] + jnp.sum(w_col * kc.astype(F32), axis=0, keepdims=True)
        mu = jnp.mean(h, axis=1, keepdims=True)
        hc = h - mu
        var = jnp.mean(hc * hc, axis=1, keepdims=True)
        hn = hc * lax.rsqrt(var + LN_EPS) * gain
        hb_ref[rows, :] = (jax.nn.sigmoid(om_ref[rows, :]) * hn).astype(BF16)
        return m_new

    m_fin = lax.fori_loop(0, S // L, chunk, m0_ref[0, 0][:, 0:1])
    c_ref[0, 0] = ct_ref[...].T
    n_ref[0, 0] = nrow_ref[...]
    m_ref[0, 0] = jnp.broadcast_to(m_fin, (1, LANES))


def _mlstm(u, halo, vm, om, icol, fcol, irow, frow, w_conv, b_conv, wq, wk, wkt, mnorm_g,
           c0, n0, m0, B, S):
    L = M_CHUNK
    nc = S // L
    H, dh = M_HEADS, M_HEAD_DIM
    seq = pl.BlockSpec((S, dh), lambda b, h: (b, h))
    col = pl.BlockSpec((1, 1, S, 1), lambda b, h: (b, h, 0, 0))
    rowg = pl.BlockSpec((1, 1, nc, L), lambda b, h: (b, h, 0, 0))
    wmat = pl.BlockSpec((1, dh, dh), lambda b, h: (h, 0, 0))
    state = pl.BlockSpec((1, 1, dh, dh), lambda b, h: (b, h, 0, 0))
    nvec = pl.BlockSpec((1, 1, 1, dh), lambda b, h: (b, h, 0, 0))
    mvec = pl.BlockSpec((1, 1, 1, LANES), lambda b, h: (b, h, 0, 0))
    return pl.pallas_call(
        functools.partial(_mlstm_kernel, S=S, L=L),
        grid=(B, H),
        in_specs=[seq,
                  pl.BlockSpec((1, CONV_HALO, dh), lambda b, h: (b, 0, h)),
                  seq, seq, col, col, rowg, rowg,
                  pl.BlockSpec((M_CONV, dh), lambda b, h: (0, h)),
                  pl.BlockSpec((1, dh), lambda b, h: (0, h)),
                  wmat, wmat, wmat,
                  pl.BlockSpec((1, dh), lambda b, h: (0, h)),
                  state, nvec, mvec],
        out_specs=[seq, state, nvec, mvec],
        out_shape=[jax.ShapeDtypeStruct((B * S, D_MODEL), BF16),
                   jax.ShapeDtypeStruct((B, H, dh, dh), F32),
                   jax.ShapeDtypeStruct((B, H, 1, dh), F32),
                   jax.ShapeDtypeStruct((B, H, 1, LANES), F32)],
        scratch_shapes=[pltpu.VMEM((S + CONV_HALO, dh), F32),
                        pltpu.VMEM((S, dh), BF16), pltpu.VMEM((S, dh), BF16), pltpu.VMEM((S, dh), BF16),
                        pltpu.VMEM((dh, dh), F32), pltpu.VMEM((1, dh), F32)],
        compiler_params=_cparams(("arbitrary", "arbitrary")),
        name="mlstm",
    )(u, halo, vm, om, icol, fcol, irow, frow, w_conv, b_conv, wq, wk, wkt, mnorm_g, c0, n0, m0)


def _layer_norm_rows(z, g, b):
    mu = jnp.mean(z, axis=1, keepdims=True)
    zc = z - mu
    var = jnp.mean(zc * zc, axis=1, keepdims=True)
    return zc * lax.rsqrt(var + LN_EPS) * g + b


def _merge_kernel(x_ref, oa_ref, hb_ref, ga_ref, gb_ref, wa_ref, wb_ref, wo_ref, g_ref, b_ref,
                  h1t_ref, h1tb_ref):
    ya = _dot(oa_ref[...], wa_ref[...])
    yb = _dot(hb_ref[...], wb_ref[...])
    merged = jax.nn.sigmoid(ga_ref[...]) * ya + jax.nn.sigmoid(gb_ref[...]) * yb
    out = _dot(merged.astype(BF16), wo_ref[...])
    h1 = _layer_norm_rows(ALPHA * x_ref[...] + out, g_ref[...], b_ref[...])
    h1t = h1.T
    h1t_ref[...] = h1t
    h1tb_ref[...] = h1t.astype(BF16)


def _merge(x, oa, hb, ga, gb, wa, wb, wo, ln_g, ln_b):
    T = x.shape[0]
    tm = min(T, 512)
    tok = pl.BlockSpec((tm, D_MODEL), lambda i: (i, 0))
    wsp = pl.BlockSpec((D_MODEL, D_MODEL), lambda i: (0, 0))
    vsp = pl.BlockSpec((1, D_MODEL), lambda i: (0, 0))
    tsp = pl.BlockSpec((D_MODEL, tm), lambda i: (0, i))
    return pl.pallas_call(
        _merge_kernel,
        grid=(T // tm,),
        in_specs=[tok, tok, tok, tok, tok, wsp, wsp, wsp, vsp, vsp],
        out_specs=[tsp, tsp],
        out_shape=[jax.ShapeDtypeStruct((D_MODEL, T), F32), jax.ShapeDtypeStruct((D_MODEL, T), BF16)],
        compiler_params=_cparams(("arbitrary",)),
        name="merge_ln1",
    )(x, oa, hb, ga, gb, wa, wb, wo, ln_g, ln_b)


NOT_SELECTED = 255.0


def _top16_ranks(s, iota):
    rank = jnp.full(s.shape, NOT_SELECTED, F32)
    tops = []
    for r in range(P_TOPK):
        mx = jnp.max(s, axis=0, keepdims=True)
        idx = jnp.min(jnp.where(s == mx, iota, float(P_NKEYS)), axis=0, keepdims=True)
        sel = iota == idx
        rank = jnp.where(sel, float(r), rank)
        s = jnp.where(sel, -jnp.inf, s)
        tops.append(mx)
    return rank, jnp.concatenate(tops, axis=0)


def _pair_merge(sv1, sv2, riota):
    cnt = jnp.zeros(sv1.shape, F32)
    front = sv1 + sv2[0:1, :]
    top0 = front[0:1, :]
    z = jnp.zeros_like(top0)
    for _ in range(P_TOPK):
        fm = jnp.max(front, axis=0, keepdims=True)
        idx = jnp.min(jnp.where(front == fm, riota, float(P_TOPK)), axis=0, keepdims=True)
        sel = riota == idx
        z = z + jnp.exp(fm - top0)
        cnt = cnt + jnp.where(sel, 1.0, 0.0)
        csel = jnp.sum(jnp.where(sel, cnt, 0.0), axis=0, keepdims=True)
        v1 = jnp.sum(jnp.where(sel, sv1, 0.0), axis=0, keepdims=True)
        v2 = jnp.sum(jnp.where(riota == csel, sv2, 0.0), axis=0, keepdims=True)
        nxt = jnp.where(csel < float(P_TOPK), v1 + v2, -jnp.inf)
        front = jnp.where(sel, nxt, front)
    return cnt, z


def _route_kernel(h1t_ref, wpqt_ref, keys_ref, e1_ref, lf_ref, e2_ref, r2_ref, qt_ref, *, tt):
    qt_ref[...] = _dot(wpqt_ref[...], h1t_ref[...]).astype(BF16)
    kiota = lax.broadcasted_iota(jnp.int32, (P_NKEYS, LANES), 0).astype(F32)
    riota = lax.broadcasted_iota(jnp.int32, (P_TOPK, LANES), 0).astype(F32)

    def head(h, carry):
        for w0 in range(0, tt, LANES):
            ws = slice(w0, w0 + LANES)
            r1 = pl.multiple_of(h * 2 * P_HALF_DIM, P_HALF_DIM)
            r2 = pl.multiple_of(h * 2 * P_HALF_DIM + P_HALF_DIM, P_HALF_DIM)
            s1 = _dot(keys_ref[2 * h], qt_ref[pl.ds(r1, P_HALF_DIM), ws])
            s2 = _dot(keys_ref[2 * h + 1], qt_ref[pl.ds(r2, P_HALF_DIM), ws])
            rank1, sv1 = _top16_ranks(s1, kiota)
            rank2, sv2 = _top16_ranks(s2, kiota)
            cnt, z = _pair_merge(sv1, sv2, riota)
            in1 = rank1 < float(P_TOPK)
            in2 = rank2 < float(P_TOPK)
            e1 = jnp.where(in1, jnp.exp(s1 - sv1[0:1, :]) / z, 0.0)
            e2 = jnp.where(in2, jnp.exp(s2 - sv2[0:1, :]), 0.0)
            lf = jnp.zeros_like(s1)
            for r in range(P_TOPK):
                lf = jnp.where(rank1 == float(r), cnt[r:r + 1, :], lf)
            e1_ref[h, :, ws] = e1
            lf_ref[h, :, ws] = lf
            e2_ref[h, :, ws] = e2.astype(BF16)
            r2_ref[h, :, ws] = rank2.astype(BF16)
        return carry

    lax.fori_loop(0, P_HEADS, head, 0)


def _route(h1tb, wpqt, keys):
    T = h1tb.shape[1]
    tt = min(T, 512)
    osp = pl.BlockSpec((P_HEADS, P_NKEYS, tt), lambda i: (0, 0, i))
    of32 = jax.ShapeDtypeStruct((P_HEADS, P_NKEYS, T), F32)
    obf = jax.ShapeDtypeStruct((P_HEADS, P_NKEYS, T), BF16)
    return pl.pallas_call(
        functools.partial(_route_kernel, tt=tt),
        grid=(T // tt,),
        in_specs=[pl.BlockSpec((D_MODEL, tt), lambda i: (0, i)),
                  pl.BlockSpec((2 * P_HEADS * P_HALF_DIM, D_MODEL), lambda i: (0, 0)),
                  pl.BlockSpec((2 * P_HEADS, P_NKEYS, P_HALF_DIM), lambda i: (0, 0, 0))],
        out_specs=[osp, osp, osp, osp],
        out_shape=[of32, of32, obf, obf],
        scratch_shapes=[pltpu.VMEM((2 * P_HEADS * P_HALF_DIM, tt), BF16)],
        compiler_params=_cparams(("arbitrary",)),
        name="peer_route",
    )(h1tb, wpqt, keys)


PEER_EXPERT_BLOCK = 1024


def _peer_kernel(h1t_ref, h1tb_ref, e1_ref, lf_ref, e2_ref, r2_ref, pu_ref, pvt_ref, g_ref, b_ref,
                 y_ref, acc_ref, w_ref, *, n_eb):
    eb = pl.program_id(1)
    rows_per_block = PEER_EXPERT_BLOCK // P_NKEYS

    @pl.when(eb == 0)
    def _():
        acc_ref[...] = jnp.zeros(acc_ref.shape, F32)

    def one_row(ci, carry):
        c = eb * rows_per_block + ci
        r0 = pl.multiple_of(ci * P_NKEYS, P_NKEYS)
        a = _dot(pu_ref[pl.ds(r0, P_NKEYS), :], h1tb_ref[...])
        act = (0.5 * a * (1.0 + lax.erf(a * (2.0 ** -0.5)))).astype(BF16)
        gate = jnp.zeros(act.shape, BF16)
        for h in range(P_HEADS):
            lrow = lf_ref[h, pl.ds(c, 1), :].astype(BF16)
            erow = e1_ref[h, pl.ds(c, 1), :].astype(BF16)
            picked = jnp.where(r2_ref[h] < lrow, e2_ref[h], jnp.zeros_like(gate))
            gate = gate + picked * erow
        w_ref[pl.ds(r0, P_NKEYS), :] = gate * act
        return carry

    lax.fori_loop(0, rows_per_block, one_row, 0)
    acc_ref[...] += _dot(pvt_ref[...], w_ref[...])

    @pl.when(eb == n_eb - 1)
    def _():
        z = (ALPHA * h1t_ref[...] + acc_ref[...]).T
        y_ref[...] = _layer_norm_rows(z, g_ref[...], b_ref[...])


def _peer(h1t, h1tb, e1, lf, e2, r2, pu, pvt, ln_g, ln_b):
    T = h1t.shape[1]
    tt = min(T, 512)
    n_eb = P_EXPERTS // PEER_EXPERT_BLOCK
    tsp = pl.BlockSpec((D_MODEL, tt), lambda i, e: (0, i))
    rsp = pl.BlockSpec((P_HEADS, P_NKEYS, tt), lambda i, e: (0, 0, i))
    vsp = pl.BlockSpec((1, D_MODEL), lambda i, e: (0, 0))
    return pl.pallas_call(
        functools.partial(_peer_kernel, n_eb=n_eb),
        grid=(T // tt, n_eb),
        in_specs=[tsp, tsp, rsp, rsp, rsp, rsp,
                  pl.BlockSpec((PEER_EXPERT_BLOCK, D_MODEL), lambda i, e: (e, 0)),
                  pl.BlockSpec((D_MODEL, PEER_EXPERT_BLOCK), lambda i, e: (0, e)),
                  vsp, vsp],
        out_specs=pl.BlockSpec((tt, D_MODEL), lambda i, e: (i, 0)),
        out_shape=jax.ShapeDtypeStruct((T, D_MODEL), F32),
        scratch_shapes=[pltpu.VMEM((D_MODEL, tt), F32), pltpu.VMEM((PEER_EXPERT_BLOCK, tt), BF16)],
        compiler_params=_cparams(("arbitrary", "arbitrary")),
        name="peer_dense",
    )(h1t, h1tb, e1, lf, e2, r2, pu, pvt, ln_g, ln_b)


def _gate_layouts(g, B, S_real, S_pad):
    H, L = M_HEADS, M_CHUNK
    gi = g[:, :H].reshape(B, S_real, H)
    gf = g[:, H:2 * H].reshape(B, S_real, H)
    if S_pad != S_real:
        pw = ((0, 0), (0, S_pad - S_real), (0, 0))
        gi = jnp.pad(gi, pw, constant_values=PAD_LOG_INPUT_GATE)
        gf = jnp.pad(gf, pw, constant_values=-PAD_LOG_INPUT_GATE)
    gi = jnp.swapaxes(gi, 1, 2)
    gf = jnp.swapaxes(gf, 1, 2)
    col = lambda a: a.reshape(B, H, S_pad, 1)
    row = lambda a: a.reshape(B, H, S_pad // L, L)
    return col(gi), col(gf), row(gi), row(gf)


def _pad_seq(a, B, S_real, S_pad):
    if S_pad == S_real:
        return a
    a = a.reshape(B, S_real, a.shape[-1])
    return jnp.pad(a, ((0, 0), (0, S_pad - S_real), (0, 0))).reshape(B * S_pad, a.shape[-1])


def _group(x3, attend, state, W):
    B, S, _ = x3.shape
    T = B * S
    x = x3.reshape(T, D_MODEL)
    q, k, kb, v, vb, u, vm, om, ga, gb, g = _in_proj(x, W["w_main"], W["b_main"], W["w_gate"], W["b_gate"])
    oa = attend(q, k, v, kb, vb)

    c0, n0, m0, conv0 = state
    S_pad = -(-S // M_CHUNK) * M_CHUNK
    icol, fcol, irow, frow = _gate_layouts(g, B, S, S_pad)
    halo = jnp.pad(conv0, ((0, 0), (CONV_HALO - (M_CONV - 1), 0), (0, 0)))
    hb, c_new, n_new, m_new = _mlstm(
        _pad_seq(u, B, S, S_pad), halo, _pad_seq(vm, B, S, S_pad), _pad_seq(om, B, S, S_pad),
        icol, fcol, irow, frow, W["w_conv"], W["b_conv"], W["w_qm"], W["w_km"], W["w_kmt"], W["mnorm_g"],
        c0, n0.reshape(B, M_HEADS, 1, M_HEAD_DIM),
        jnp.broadcast_to(m0[:, :, None, None], (B, M_HEADS, 1, LANES)), B, S_pad)
    if S_pad != S:
        hb = hb.reshape(B, S_pad, D_MODEL)[:, :S].reshape(T, D_MODEL)
    u3 = u.reshape(B, S, D_MODEL)
    conv_new = jnp.concatenate([conv0, u3], axis=1)[:, S:]

    h1t, h1tb = _merge(x, oa, hb, ga, gb, W["w_a"], W["w_b"], W["w_o"], W["ln1_g"], W["ln1_b"])
    e1, lf, e2, r2 = _route(h1tb, W["w_pqt"], W["p_keys"])
    y = _peer(h1t, h1tb, e1, lf, e2, r2, W["p_u"], W["p_vt"], W["ln2_g"], W["ln2_b"])

    new_state = (k.reshape(1, B, S, A_HEADS, A_HD2), v.reshape(1, B, S, A_HEADS, A_HD2),
                 c_new[None], n_new.reshape(1, B, M_HEADS, M_HEAD_DIM), m_new[None, :, :, 0, 0],
                 conv_new[None])
    return y.reshape(B, S, D_MODEL), new_state


def kernel(x_prompt, x_sample, cache_k, cache_v, state_C, state_n, state_m, state_conv, page_table,
           w_in, b_in, lam_q1, lam_k1, lam_q2, lam_k2, subln_g, w_conv, b_conv, w_qm, w_km, mnorm_g,
           w_a, w_b, w_o, ln1_g, ln1_b, w_pq, p_keys, p_u, p_v, ln2_g, ln2_b):
    assert w_in.shape[0] == DEPTH
    B, S, _ = x_prompt.shape
    DB, DS, _ = x_sample.shape
    assert 2 * DS == SAMPLE_ROWS

    n_pre = 6 * D_MODEL
    wi, bi = w_in[0], b_in[0]
    W = {
        "w_main": jnp.concatenate([wi[:, :n_pre], wi[:, n_pre + GATE_COLS:]], axis=1).astype(BF16),
        "b_main": jnp.concatenate([bi[:n_pre], bi[n_pre + GATE_COLS:]])[None],
        "w_gate": jnp.pad(wi[:, n_pre:n_pre + GATE_COLS], ((0, 0), (0, LANES - GATE_COLS))).astype(BF16),
        "b_gate": jnp.pad(bi[n_pre:n_pre + GATE_COLS], (0, LANES - GATE_COLS))[None],
        "w_conv": w_conv[0], "b_conv": b_conv[0][None],
        "w_qm": w_qm[0].astype(BF16), "w_km": w_km[0].astype(BF16),
        "w_kmt": jnp.swapaxes(w_km[0], 1, 2).astype(BF16),
        "mnorm_g": mnorm_g[0][None],
        "w_a": w_a[0].astype(BF16), "w_b": w_b[0].astype(BF16), "w_o": w_o[0].astype(BF16),
        "ln1_g": ln1_g[0][None], "ln1_b": ln1_b[0][None],
        "w_pqt": w_pq[0].T.astype(BF16),
        "p_keys": p_keys[0].reshape(2 * P_HEADS, P_NKEYS, P_HALF_DIM).astype(BF16),
        "p_u": p_u[0].astype(BF16), "p_vt": p_v[0].T.astype(BF16),
        "ln2_g": ln2_g[0][None], "ln2_b": ln2_b[0][None],
    }
    lam_vecs = (lam_q1[0][None], lam_k1[0][None], lam_q2[0][None], lam_k2[0][None])
    sg = subln_g[0][None]

    def attend_prompt(q, k, v, kb, vb):
        return _attn_prompt(q, kb, vb, lam_vecs, sg, B, S)

    def attend_sample(q, k, v, kb, vb):
        q3 = q.reshape(DB, DS, D_MODEL)
        q8 = jnp.concatenate([q3, q3], axis=1)
        padn = ((0, 0), (0, SAMPLE_ROWS - DS), (0, 0))
        kn8 = jnp.pad(k.reshape(DB, DS, D_MODEL), padn)
        vn8 = jnp.pad(v.reshape(DB, DS, D_MODEL), padn)
        n_pool = cache_k.shape[1]
        ck = cache_k[0].reshape(n_pool, PAGE_SIZE, D_MODEL)
        cv = cache_v[0].reshape(n_pool, PAGE_SIZE, D_MODEL)
        o8 = _attn_sample(q8, kn8, vn8, ck, cv, page_table, lam_vecs, sg)
        return o8[:, :DS].reshape(DB * DS, D_MODEL).astype(BF16)

    zero_state = (jnp.zeros((B, M_HEADS, M_HEAD_DIM, M_HEAD_DIM), F32),
                  jnp.zeros((B, M_HEADS, M_HEAD_DIM), F32),
                  jnp.zeros((B, M_HEADS), F32),
                  jnp.zeros((B, M_CONV - 1, D_MODEL), F32))
    yp, sp = _group(x_prompt, attend_prompt, zero_state, W)
    ys, ss = _group(x_sample, attend_sample,
                    (state_C[0], state_n[0], state_m[0], state_conv[0]), W)
    return (yp, ys) + sp + ss
```

```python
import functools
import math

import jax
import jax.numpy as jnp
from jax import lax
from jax.experimental import pallas as pl
from jax.experimental.pallas import tpu as pltpu

F32 = jnp.float32
BF16 = jnp.bfloat16

D_MODEL = 1024
A_HEADS = 8
A_HEAD_DIM = 64
A_HD2 = 2 * A_HEAD_DIM
M_HEADS = 4
M_HEAD_DIM = 256
M_CONV = 4
M_CHUNK = 64
P_HEADS = 8
P_NKEYS = 128
P_HALF_DIM = 128
P_TOPK = 16
P_EXPERTS = P_NKEYS * P_NKEYS
PAGE_SIZE = 128
DEPTH = 1
ALPHA = (2.0 * DEPTH) ** 0.25
LN_EPS = 1e-5
PAD_LOG_INPUT_GATE = -1e30
LAM_INIT = 0.8 - 0.6 * math.exp(-0.3 * 0)

N_MAIN_GROUPS = 8
GATE_COLS = 2 * M_HEADS
LANES = 128
VMEM_LIMIT = 56 * 1024 * 1024

NT_DIMS = (((1,), (1,)), ((), ()))


def _cparams(sem):
    return pltpu.CompilerParams(dimension_semantics=sem, vmem_limit_bytes=VMEM_LIMIT)


def _dot(a, b):
    return jnp.dot(a, b, preferred_element_type=F32)


def _dot_nt(a, b):
    return lax.dot_general(a, b, NT_DIMS, preferred_element_type=F32)


def _in_proj_kernel(x_ref, w_ref, b_ref, wg_ref, bg_ref,
                    q_ref, k_ref, kb_ref, v_ref, vb_ref, u_ref, vm_ref, om_ref, ga_ref, gb_ref,
                    g_ref):
    j = pl.program_id(1)

    def proj():
        return _dot(x_ref[...].astype(BF16), w_ref[...]) + b_ref[...]

    @pl.when(j == 0)
    def _():
        q_ref[...] = (proj() * (A_HEAD_DIM ** -0.5)).astype(BF16)
        g_ref[...] = _dot(x_ref[...].astype(BF16), wg_ref[...]) + bg_ref[...]

    @pl.when(j == 1)
    def _():
        k_ref[...] = proj()
        kb_ref[...] = k_ref[...].astype(BF16)

    @pl.when(j == 2)
    def _():
        v_ref[...] = proj()
        vb_ref[...] = v_ref[...].astype(BF16)

    for idx, ref in ((3, u_ref), (5, om_ref), (6, ga_ref), (7, gb_ref)):
        @pl.when(j == idx)
        def _(ref=ref):
            ref[...] = proj()

    @pl.when(j == 4)
    def _():
        vm_ref[...] = proj().astype(BF16)


def _in_proj(x, w_main, b_main, w_gate, b_gate):
    T = x.shape[0]
    tm = min(T, 512)
    assert T % tm == 0
    row = lambda i, j: (i, 0)
    f32o = jax.ShapeDtypeStruct((T, D_MODEL), F32)
    bf16o = jax.ShapeDtypeStruct((T, D_MODEL), BF16)
    ospec = pl.BlockSpec((tm, D_MODEL), row)
    return pl.pallas_call(
        _in_proj_kernel,
        grid=(T // tm, N_MAIN_GROUPS),
        in_specs=[
            pl.BlockSpec((tm, D_MODEL), row),
            pl.BlockSpec((D_MODEL, D_MODEL), lambda i, j: (0, j)),
            pl.BlockSpec((1, D_MODEL), lambda i, j: (0, j)),
            pl.BlockSpec((D_MODEL, LANES), lambda i, j: (0, 0)),
            pl.BlockSpec((1, LANES), lambda i, j: (0, 0)),
        ],
        out_specs=[ospec] * 10 + [pl.BlockSpec((tm, LANES), row)],
        out_shape=[bf16o, f32o, bf16o, f32o, bf16o, f32o, bf16o, f32o, f32o, f32o,
                   jax.ShapeDtypeStruct((T, LANES), F32)],
        compiler_params=_cparams(("arbitrary", "arbitrary")),
        name="in_proj",
    )(x, w_main, b_main, w_gate, b_gate)


def _lam_value(lq1, lk1, lq2, lk2):
    a = jnp.exp(jnp.sum(lq1 * lk1, axis=1, keepdims=True))
    b = jnp.exp(jnp.sum(lq2 * lk2, axis=1, keepdims=True))
    return a - b + LAM_INIT


def _sub_norm(o, g):
    r = lax.rsqrt(jnp.mean(o * o, axis=1, keepdims=True) + LN_EPS)
    return o * r * g * (1.0 - LAM_INIT)


ATTN_TQ = 256


def _attn_prompt_kernel(lq1_ref, lk1_ref, lq2_ref, lk2_ref, gcol_ref, q_ref, k_ref, vt_ref, o_ref, *, nq):
    i = pl.program_id(2)
    tq = ATTN_TQ
    lam = _lam_value(lq1_ref[...], lk1_ref[...], lq2_ref[...], lk2_ref[...])
    q = q_ref[...]
    lane = lax.broadcasted_iota(jnp.int32, q.shape, 1)
    zero = jnp.zeros_like(q)
    maps = (jnp.where(lane < A_HEAD_DIM, q, zero), jnp.where(lane >= A_HEAD_DIM, q, zero))
    krow = lax.broadcasted_iota(jnp.int32, (tq, tq), 0)
    qcol = lax.broadcasted_iota(jnp.int32, (tq, tq), 1)

    def one_map(qm, n):
        lo, hi = n * tq, (n + 1) * tq
        sd = jnp.where(krow <= qcol, _dot_nt(k_ref[lo:hi, :], qm), -jnp.inf)
        m = jnp.max(sd, axis=0, keepdims=True)
        if n:
            st = _dot_nt(k_ref[0:lo, :], qm)
            m = jnp.maximum(m, jnp.max(st, axis=0, keepdims=True))
        pd = jnp.exp(sd - m)
        l = jnp.sum(pd, axis=0, keepdims=True)
        acc = _dot(vt_ref[:, lo:hi], pd.astype(BF16))
        if n:
            pt = jnp.exp(st - m)
            l = l + jnp.sum(pt, axis=0, keepdims=True)
            acc = acc + _dot(vt_ref[:, 0:lo], pt.astype(BF16))
        return acc / l

    for n in range(nq):
        @pl.when(i == n)
        def _(n=n):
            o = one_map(maps[0], n) - lam * one_map(maps[1], n)
            r = lax.rsqrt(jnp.mean(o * o, axis=0, keepdims=True) + LN_EPS)
            o = o * r * gcol_ref[...] * (1.0 - LAM_INIT)
            o_ref[...] = o.T.astype(BF16)


def _attn_prompt(q, kb, vbt, lam_vecs, subln_gcol, B, S):
    tq = ATTN_TQ
    assert S % tq == 0
    nq = S // tq
    vec = pl.BlockSpec((1, A_HEAD_DIM), lambda b, h, i: (0, 0))
    return pl.pallas_call(
        functools.partial(_attn_prompt_kernel, nq=nq),
        grid=(B, A_HEADS, nq),
        in_specs=[vec, vec, vec, vec,
                  pl.BlockSpec((A_HD2, 1), lambda b, h, i: (0, 0)),
                  pl.BlockSpec((tq, A_HD2), lambda b, h, i: (b * nq + i, h)),
                  pl.BlockSpec((S, A_HD2), lambda b, h, i: (b, h)),
                  pl.BlockSpec((A_HD2, S), lambda b, h, i: (h, b))],
        out_specs=pl.BlockSpec((tq, A_HD2), lambda b, h, i: (b * nq + i, h)),
        out_shape=jax.ShapeDtypeStruct((B * S, D_MODEL), BF16),
        compiler_params=_cparams(("arbitrary", "arbitrary", "arbitrary")),
        name="attn_prompt",
    )(*lam_vecs, subln_gcol, q, kb, vbt)


SAMPLE_ROWS = 8
PAGES_PER_STEP = 4


def _attn_sample_kernel(pt_ref, lq1_ref, lk1_ref, lq2_ref, lk2_ref, g_ref, q_ref, kn_ref, vn_ref, *rest,
                        n_steps, dec_seq):
    pp = PAGES_PER_STEP
    k_refs, v_refs = rest[:pp], rest[pp:2 * pp]
    o_ref, qf_ref, bias_ref, m_ref, l_ref, acc_ref = rest[2 * pp:]
    step_id = pl.program_id(1)
    rows = A_HEADS * SAMPLE_ROWS
    flat = PAGE_SIZE * A_HEADS

    @pl.when(step_id == 0)
    def _():
        q8 = q_ref[0]
        qf = jnp.concatenate([q8[:, h * A_HD2:(h + 1) * A_HD2] for h in range(A_HEADS)], axis=0)
        row = lax.broadcasted_iota(jnp.int32, qf.shape, 0)
        lane = lax.broadcasted_iota(jnp.int32, qf.shape, 1)
        first_map = (row % SAMPLE_ROWS) < dec_seq
        qf_ref[...] = jnp.where(first_map == (lane < A_HEAD_DIM), qf, jnp.zeros_like(qf))
        brow = lax.broadcasted_iota(jnp.int32, (rows, flat), 0)
        bcol = lax.broadcasted_iota(jnp.int32, (rows, flat), 1)
        bias_ref[...] = jnp.where((bcol % A_HEADS) == (brow // SAMPLE_ROWS), 0.0, -jnp.inf)
        m_ref[...] = jnp.full(m_ref.shape, -jnp.inf, F32)
        l_ref[...] = jnp.zeros(l_ref.shape, F32)
        acc_ref[...] = jnp.zeros(acc_ref.shape, F32)

    def update(kfs, vfs, bias):
        ss = [_dot_nt(qf_ref[...], kf) + bias for kf in kfs]
        m = m_ref[...]
        m_new = m
        for s in ss:
            m_new = jnp.maximum(m_new, jnp.max(s, axis=1, keepdims=True))
        alpha = jnp.exp(m - m_new)
        l = alpha * l_ref[...]
        acc = alpha * acc_ref[...]
        for s, vf in zip(ss, vfs):
            p = jnp.exp(s - m_new)
            l = l + jnp.sum(p, axis=1, keepdims=True)
            acc = acc + _dot(p.astype(BF16), vf)
        l_ref[...] = l
        acc_ref[...] = acc
        m_ref[...] = m_new

    update([k_refs[t][0, 0].reshape(flat, A_HD2).astype(BF16) for t in range(pp)],
           [v_refs[t][0, 0].reshape(flat, A_HD2).astype(BF16) for t in range(pp)], bias_ref[...])

    @pl.when(step_id == n_steps - 1)
    def _():
        nnew = SAMPLE_ROWS * A_HEADS
        pad = jnp.zeros((PAGE_SIZE - nnew, A_HD2), BF16)
        kf = jnp.concatenate([kn_ref[0].reshape(nnew, A_HD2).astype(BF16), pad], axis=0)
        vf = jnp.concatenate([vn_ref[0].reshape(nnew, A_HD2).astype(BF16), pad], axis=0)
        row = lax.broadcasted_iota(jnp.int32, (rows, PAGE_SIZE), 0)
        col = lax.broadcasted_iota(jnp.int32, (rows, PAGE_SIZE), 1)
        ok = ((col % A_HEADS) == (row // SAMPLE_ROWS)) & ((col // A_HEADS) <= (row % dec_seq))
        update([kf], [vf], jnp.where(ok, 0.0, -jnp.inf))
        lam = _lam_value(lq1_ref[...], lk1_ref[...], lq2_ref[...], lk2_ref[...])
        g = g_ref[...]
        for h in range(A_HEADS):
            rs = slice(h * SAMPLE_ROWS, (h + 1) * SAMPLE_ROWS)
            o = acc_ref[rs, :] / l_ref[rs, :]
            o = o - lam * pltpu.roll(o, SAMPLE_ROWS - dec_seq, 0)
            o_ref[0, :, h * A_HD2:(h + 1) * A_HD2] = _sub_norm(o, g)


def _attn_sample(q8, kn8, vn8, cache_k, cache_v, page_table, lam_vecs, subln_g):
    DB, n_pages = page_table.shape
    pp = PAGES_PER_STEP
    assert n_pages % pp == 0
    n_steps = n_pages // pp
    dec_seq = SAMPLE_ROWS // 2
    rows = A_HEADS * SAMPLE_ROWS
    vec = pl.BlockSpec((1, A_HEAD_DIM), lambda b, s, pt: (0, 0))
    per_b = pl.BlockSpec((1, SAMPLE_ROWS, D_MODEL), lambda b, s, pt: (b, 0, 0))
    new_kv = pl.BlockSpec((1, SAMPLE_ROWS, A_HEADS, A_HD2), lambda b, s, pt: (b, 0, 0, 0))

    def page_spec(t):
        return pl.BlockSpec((1, 1, PAGE_SIZE, A_HEADS, A_HD2),
                            lambda b, s, pt: (0, pt[b, s * pp + t], 0, 0, 0))

    grid_spec = pltpu.PrefetchScalarGridSpec(
        num_scalar_prefetch=1,
        grid=(DB, n_steps),
        in_specs=[vec, vec, vec, vec, pl.BlockSpec((1, A_HD2), lambda b, s, pt: (0, 0)),
                  per_b, new_kv, new_kv]
                 + [page_spec(t) for t in range(pp)] + [page_spec(t) for t in range(pp)],
        out_specs=per_b,
        scratch_shapes=[pltpu.VMEM((rows, A_HD2), BF16),
                        pltpu.VMEM((rows, PAGE_SIZE * A_HEADS), F32),
                        pltpu.VMEM((rows, 1), F32), pltpu.VMEM((rows, 1), F32),
                        pltpu.VMEM((rows, A_HD2), F32)],
    )
    return pl.pallas_call(
        functools.partial(_attn_sample_kernel, n_steps=n_steps, dec_seq=dec_seq),
        grid_spec=grid_spec,
        out_shape=jax.ShapeDtypeStruct((DB, SAMPLE_ROWS, D_MODEL), F32),
        compiler_params=_cparams(("arbitrary", "arbitrary")),
        name="attn_sample",
    )(page_table, *lam_vecs, subln_g, q8, kn8, vn8, *([cache_k] * pp), *([cache_v] * pp))


CONV_HALO = 8


def _log_sigmoid(x):
    return -(jnp.maximum(-x, 0.0) + jnp.log1p(jnp.exp(-jnp.abs(x))))


def _mlstm_kernel(u_ref, halo_ref, vm_ref, om_ref, icol_ref, fcol_ref, irow_ref, frow_ref,
                  wconv_ref, bconv_ref, wq_ref, wk_ref, wkt_ref, g_ref, c0_ref, n0_ref, m0_ref,
                  hb_ref, c_ref, n_ref, m_ref,
                  ext_ref, uc_ref, q_ref, k_ref, ct_ref, nrow_ref, *, S, L):
    kscale = M_HEAD_DIM ** -0.5
    ext_ref[0:CONV_HALO, :] = halo_ref[0]
    ext_ref[CONV_HALO:CONV_HALO + S, :] = u_ref[...]
    rb = min(S, 512)
    for r0 in range(0, S, rb):
        acc = jnp.broadcast_to(bconv_ref[...], (rb, M_HEAD_DIM))
        for jj in range(M_CONV):
            off = CONV_HALO - (M_CONV - 1) + jj + r0
            acc = acc + ext_ref[off:off + rb, :] * wconv_ref[jj:jj + 1, :]
        uc = (acc * jax.nn.sigmoid(acc)).astype(BF16)
        uc_ref[r0:r0 + rb, :] = uc
        q_ref[r0:r0 + rb, :] = _dot(uc, wq_ref[0]).astype(BF16)
        k_ref[r0:r0 + rb, :] = (_dot(uc, wk_ref[0]) * kscale).astype(BF16)

    ct_ref[...] = c0_ref[0, 0].T
    nrow_ref[...] = n0_ref[0, 0]
    ti = lax.broadcasted_iota(jnp.int32, (L, L), 0)
    si = lax.broadcasted_iota(jnp.int32, (L, L), 1)
    tril = si <= ti
    gain = g_ref[...]

    def chunk(c, m_prev):
        r0 = pl.multiple_of(c * L, L)
        rows = pl.ds(r0, L)
        qc = q_ref[rows, :]
        kc = k_ref[rows, :]
        vc = vm_ref[rows, :]
        kct = (_dot_nt(wkt_ref[0], uc_ref[rows, :]) * kscale).astype(BF16)
        icol = icol_ref[0, 0, rows, :]
        fcol = _log_sigmoid(fcol_ref[0, 0, rows, :])
        irow = irow_ref[0, 0, pl.ds(c, 1), :]
        frow = _log_sigmoid(frow_ref[0, 0, pl.ds(c, 1), :])
        b_col = jnp.sum(jnp.where(tril, frow, 0.0), axis=1, keepdims=True)
        b_row = jnp.sum(jnp.where(ti <= si, fcol, 0.0), axis=0, keepdims=True)
        dmat = jnp.where(tril, b_col - b_row + irow, -jnp.inf)
        m_t = jnp.maximum(b_col + m_prev, jnp.max(dmat, axis=1, keepdims=True))
        wmat = jnp.exp(dmat - m_t)
        inter = jnp.exp(b_col + m_prev - m_t)
        sw = _dot(qc, kct) * wmat
        ctb = ct_ref[...].astype(BF16)
        num = inter * _dot(qc, ctb) + _dot(sw.astype(BF16), vc)
        qn = jnp.sum(qc.astype(F32) * nrow_ref[...], axis=1, keepdims=True)
        dot = inter * qn + jnp.sum(sw, axis=1, keepdims=True)
        h = num / jnp.maximum(jnp.abs(dot), jnp.exp(-m_t))
        m_new = m_t[L - 1:L, :]
        b_last = b_col[L - 1:L, :]
        decay = jnp.exp(b_last + m_prev - m_new)
        w_col = jnp.exp(b_last - b_col + icol - m_new)
        wv = (w_col * vc.astype(F32)).astype(BF16)
        ct_ref[...] = decay * ct_ref[...] + _dot(kct, wv)
        nrow_ref[...] = decay * nrow_ref[...] + jnp.sum(w_col * kc.astype(F32), axis=0, keepdims=True)
        mu = jnp.mean(h, axis=1, keepdims=True)
        hc = h - mu
        var = jnp.mean(hc * hc, axis=1, keepdims=True)
        hn = hc * lax.rsqrt(var + LN_EPS) * gain
        hb_ref[rows, :] = (jax.nn.sigmoid(om_ref[rows, :]) * hn).astype(BF16)
        return m_new

    m_fin = lax.fori_loop(0, S // L, chunk, m0_ref[0, 0][:, 0:1])
    c_ref[0, 0] = ct_ref[...].T
    n_ref[0, 0] = nrow_ref[...]
    m_ref[0, 0] = jnp.broadcast_to(m_fin, (1, LANES))


def _mlstm(u, halo, vm, om, icol, fcol, irow, frow, w_conv, b_conv, wq, wk, wkt, mnorm_g,
           c0, n0, m0, B, S):
    L = M_CHUNK
    nc = S // L
    H, dh = M_HEADS, M_HEAD_DIM
    seq = pl.BlockSpec((S, dh), lambda b, h: (b, h))
    col = pl.BlockSpec((1, 1, S, 1), lambda b, h: (b, h, 0, 0))
    rowg = pl.BlockSpec((1, 1, nc, L), lambda b, h: (b, h, 0, 0))
    wmat = pl.BlockSpec((1, dh, dh), lambda b, h: (h, 0, 0))
    state = pl.BlockSpec((1, 1, dh, dh), lambda b, h: (b, h, 0, 0))
    nvec = pl.BlockSpec((1, 1, 1, dh), lambda b, h: (b, h, 0, 0))
    mvec = pl.BlockSpec((1, 1, 1, LANES), lambda b, h: (b, h, 0, 0))
    return pl.pallas_call(
        functools.partial(_mlstm_kernel, S=S, L=L),
        grid=(B, H),
        in_specs=[seq,
                  pl.BlockSpec((1, CONV_HALO, dh), lambda b, h: (b, 0, h)),
                  seq, seq, col, col, rowg, rowg,
                  pl.BlockSpec((M_CONV, dh), lambda b, h: (0, h)),
                  pl.BlockSpec((1, dh), lambda b, h: (0, h)),
                  wmat, wmat, wmat,
                  pl.BlockSpec((1, dh), lambda b, h: (0, h)),
                  state, nvec, mvec],
        out_specs=[seq, state, nvec, mvec],
        out_shape=[jax.ShapeDtypeStruct((B * S, D_MODEL), BF16),
                   jax.ShapeDtypeStruct((B, H, dh, dh), F32),
                   jax.ShapeDtypeStruct((B, H, 1, dh), F32),
                   jax.ShapeDtypeStruct((B, H, 1, LANES), F32)],
        scratch_shapes=[pltpu.VMEM((S + CONV_HALO, dh), F32),
                        pltpu.VMEM((S, dh), BF16), pltpu.VMEM((S, dh), BF16), pltpu.VMEM((S, dh), BF16),
                        pltpu.VMEM((dh, dh), F32), pltpu.VMEM((1, dh), F32)],
        compiler_params=_cparams(("arbitrary", "arbitrary")),
        name="mlstm",
    )(u, halo, vm, om, icol, fcol, irow, frow, w_conv, b_conv, wq, wk, wkt, mnorm_g, c0, n0, m0)


def _layer_norm_rows(z, g, b):
    mu = jnp.mean(z, axis=1, keepdims=True)
    zc = z - mu
    var = jnp.mean(zc * zc, axis=1, keepdims=True)
    return zc * lax.rsqrt(var + LN_EPS) * g + b


def _merge_kernel(x_ref, oa_ref, hb_ref, ga_ref, gb_ref, wa_ref, wb_ref, wo_ref, g_ref, b_ref,
                  h1t_ref, h1tb_ref):
    ya = _dot(oa_ref[...], wa_ref[...])
    yb = _dot(hb_ref[...], wb_ref[...])
    merged = jax.nn.sigmoid(ga_ref[...]) * ya + jax.nn.sigmoid(gb_ref[...]) * yb
    out = _dot(merged.astype(BF16), wo_ref[...])
    h1 = _layer_norm_rows(ALPHA * x_ref[...] + out, g_ref[...], b_ref[...])
    h1t = h1.T
    h1t_ref[...] = h1t
    h1tb_ref[...] = h1t.astype(BF16)


def _merge(x, oa, hb, ga, gb, wa, wb, wo, ln_g, ln_b):
    T = x.shape[0]
    tm = min(T, 512)
    tok = pl.BlockSpec((tm, D_MODEL), lambda i: (i, 0))
    wsp = pl.BlockSpec((D_MODEL, D_MODEL), lambda i: (0, 0))
    vsp = pl.BlockSpec((1, D_MODEL), lambda i: (0, 0))
    tsp = pl.BlockSpec((D_MODEL, tm), lambda i: (0, i))
    return pl.pallas_call(
        _merge_kernel,
        grid=(T // tm,),
        in_specs=[tok, tok, tok, tok, tok, wsp, wsp, wsp, vsp, vsp],
        out_specs=[tsp, tsp],
        out_shape=[jax.ShapeDtypeStruct((D_MODEL, T), F32), jax.ShapeDtypeStruct((D_MODEL, T), BF16)],
        compiler_params=_cparams(("arbitrary",)),
        name="merge_ln1",
    )(x, oa, hb, ga, gb, wa, wb, wo, ln_g, ln_b)


NOT_SELECTED = 255.0


def _top16_ranks(s, iota):
    rank = jnp.full(s.shape, NOT_SELECTED, F32)
    tops = []
    for r in range(P_TOPK):
        mx = jnp.max(s, axis=0, keepdims=True)
        idx = jnp.min(jnp.where(s == mx, iota, float(P_NKEYS)), axis=0, keepdims=True)
        sel = iota == idx
        rank = jnp.where(sel, float(r), rank)
        s = jnp.where(sel, -jnp.inf, s)
        tops.append(mx)
    return rank, jnp.concatenate(tops, axis=0)


def _pair_merge(sv1, sv2, riota):
    cnt = jnp.zeros(sv1.shape, F32)
    front = sv1 + sv2[0:1, :]
    top0 = front[0:1, :]
    z = jnp.zeros_like(top0)
    for _ in range(P_TOPK):
        fm = jnp.max(front, axis=0, keepdims=True)
        idx = jnp.min(jnp.where(front == fm, riota, float(P_TOPK)), axis=0, keepdims=True)
        sel = riota == idx
        z = z + jnp.exp(fm - top0)
        cnt = cnt + jnp.where(sel, 1.0, 0.0)
        csel = jnp.sum(jnp.where(sel, cnt, 0.0), axis=0, keepdims=True)
        v1 = jnp.sum(jnp.where(sel, sv1, 0.0), axis=0, keepdims=True)
        v2 = jnp.sum(jnp.where(riota == csel, sv2, 0.0), axis=0, keepdims=True)
        nxt = jnp.where(csel < float(P_TOPK), v1 + v2, -jnp.inf)
        front = jnp.where(sel, nxt, front)
    return cnt, z


def _route_kernel(h1t_ref, wpqt_ref, keys_ref, e1_ref, lf_ref, e2_ref, r2_ref, qt_ref, *, tt):
    qt_ref[...] = _dot(wpqt_ref[...], h1t_ref[...]).astype(BF16)
    kiota = lax.broadcasted_iota(jnp.int32, (P_NKEYS, LANES), 0).astype(F32)
    riota = lax.broadcasted_iota(jnp.int32, (P_TOPK, LANES), 0).astype(F32)

    def head(h, carry):
        for w0 in range(0, tt, LANES):
            ws = slice(w0, w0 + LANES)
            r1 = pl.multiple_of(h * 2 * P_HALF_DIM, P_HALF_DIM)
            r2 = pl.multiple_of(h * 2 * P_HALF_DIM + P_HALF_DIM, P_HALF_DIM)
            s1 = _dot(keys_ref[2 * h], qt_ref[pl.ds(r1, P_HALF_DIM), ws])
            s2 = _dot(keys_ref[2 * h + 1], qt_ref[pl.ds(r2, P_HALF_DIM), ws])
            rank1, sv1 = _top16_ranks(s1, kiota)
            rank2, sv2 = _top16_ranks(s2, kiota)
            cnt, z = _pair_merge(sv1, sv2, riota)
            in1 = rank1 < float(P_TOPK)
            in2 = rank2 < float(P_TOPK)
            e1 = jnp.where(in1, jnp.exp(s1 - sv1[0:1, :]) / z, 0.0)
            e2 = jnp.where(in2, jnp.exp(s2 - sv2[0:1, :]), 0.0)
            lf = jnp.zeros_like(s1)
            for r in range(P_TOPK):
                lf = jnp.where(rank1 == float(r), cnt[r:r + 1, :], lf)
            e1_ref[h, :, ws] = e1
            lf_ref[h, :, ws] = lf
            e2_ref[h, :, ws] = e2.astype(BF16)
            r2_ref[h, :, ws] = rank2.astype(BF16)
        return carry

    lax.fori_loop(0, P_HEADS, head, 0)


def _route(h1tb, wpqt, keys):
    T = h1tb.shape[1]
    tt = min(T, 512)
    osp = pl.BlockSpec((P_HEADS, P_NKEYS, tt), lambda i: (0, 0, i))
    of32 = jax.ShapeDtypeStruct((P_HEADS, P_NKEYS, T), F32)
    obf = jax.ShapeDtypeStruct((P_HEADS, P_NKEYS, T), BF16)
    return pl.pallas_call(
        functools.partial(_route_kernel, tt=tt),
        grid=(T // tt,),
        in_specs=[pl.BlockSpec((D_MODEL, tt), lambda i: (0, i)),
                  pl.BlockSpec((2 * P_HEADS * P_HALF_DIM, D_MODEL), lambda i: (0, 0)),
                  pl.BlockSpec((2 * P_HEADS, P_NKEYS, P_HALF_DIM), lambda i: (0, 0, 0))],
        out_specs=[osp, osp, osp, osp],
        out_shape=[of32, of32, obf, obf],
        scratch_shapes=[pltpu.VMEM((2 * P_HEADS * P_HALF_DIM, tt), BF16)],
        compiler_params=_cparams(("arbitrary",)),
        name="peer_route",
    )(h1tb, wpqt, keys)


PEER_EXPERT_BLOCK = 1024
PEER_SUB_BLOCK = 256


def _peer_kernel(h1t_ref, h1tb_ref, e1_ref, lf_ref, e2_ref, r2_ref, pu_ref, pvt_ref, g_ref, b_ref,
                 y_ref, acc_ref, *, n_eb):
    eb = pl.program_id(1)
    rows_per_block = PEER_EXPERT_BLOCK // P_NKEYS

    @pl.when(eb == 0)
    def _():
        acc_ref[...] = jnp.zeros(acc_ref.shape, F32)

    sub = PEER_SUB_BLOCK
    for sb in range(PEER_EXPERT_BLOCK // sub):
        a = _dot(pu_ref[sb * sub:(sb + 1) * sub, :], h1tb_ref[...])
        act = (0.5 * a * (1.0 + lax.erf(a * (2.0 ** -0.5)))).astype(BF16)
        gates = []
        for ci in range(sub // P_NKEYS):
            c = eb * rows_per_block + sb * (sub // P_NKEYS) + ci
            gate = jnp.zeros((P_NKEYS, act.shape[1]), BF16)
            for h in range(P_HEADS):
                lrow = lf_ref[h, pl.ds(c, 1), :].astype(BF16)
                erow = e1_ref[h, pl.ds(c, 1), :].astype(BF16)
                picked = jnp.where(r2_ref[h] < lrow, e2_ref[h], jnp.zeros_like(gate))
                gate = gate + picked * erow
            gates.append(gate)
        w = jnp.concatenate(gates, axis=0) * act
        acc_ref[...] += _dot(pvt_ref[:, sb * sub:(sb + 1) * sub], w)

    @pl.when(eb == n_eb - 1)
    def _():
        z = (ALPHA * h1t_ref[...] + acc_ref[...]).T
        y_ref[...] = _layer_norm_rows(z, g_ref[...], b_ref[...])


def _peer(h1t, h1tb, e1, lf, e2, r2, pu, pvt, ln_g, ln_b):
    T = h1t.shape[1]
    tt = min(T, 512)
    n_eb = P_EXPERTS // PEER_EXPERT_BLOCK
    tsp = pl.BlockSpec((D_MODEL, tt), lambda i, e: (0, i))
    rsp = pl.BlockSpec((P_HEADS, P_NKEYS, tt), lambda i, e: (0, 0, i))
    vsp = pl.BlockSpec((1, D_MODEL), lambda i, e: (0, 0))
    return pl.pallas_call(
        functools.partial(_peer_kernel, n_eb=n_eb),
        grid=(T // tt, n_eb),
        in_specs=[tsp, tsp, rsp, rsp, rsp, rsp,
                  pl.BlockSpec((PEER_EXPERT_BLOCK, D_MODEL), lambda i, e: (e, 0)),
                  pl.BlockSpec((D_MODEL, PEER_EXPERT_BLOCK), lambda i, e: (0, e)),
                  vsp, vsp],
        out_specs=pl.BlockSpec((tt, D_MODEL), lambda i, e: (i, 0)),
        out_shape=jax.ShapeDtypeStruct((T, D_MODEL), F32),
        scratch_shapes=[pltpu.VMEM((D_MODEL, tt), F32)],
        compiler_params=_cparams(("arbitrary", "arbitrary")),
        name="peer_dense",
    )(h1t, h1tb, e1, lf, e2, r2, pu, pvt, ln_g, ln_b)


def _gate_layouts(g, B, S_real, S_pad):
    H, L = M_HEADS, M_CHUNK
    gi = g[:, :H].reshape(B, S_real, H)
    gf = g[:, H:2 * H].reshape(B, S_real, H)
    if S_pad != S_real:
        pw = ((0, 0), (0, S_pad - S_real), (0, 0))
        gi = jnp.pad(gi, pw, constant_values=PAD_LOG_INPUT_GATE)
        gf = jnp.pad(gf, pw, constant_values=-PAD_LOG_INPUT_GATE)
    gi = jnp.swapaxes(gi, 1, 2)
    gf = jnp.swapaxes(gf, 1, 2)
    col = lambda a: a.reshape(B, H, S_pad, 1)
    row = lambda a: a.reshape(B, H, S_pad // L, L)
    return col(gi), col(gf), row(gi), row(gf)


def _pad_seq(a, B, S_real, S_pad):
    if S_pad == S_real:
        return a
    a = a.reshape(B, S_real, a.shape[-1])
    return jnp.pad(a, ((0, 0), (0, S_pad - S_real), (0, 0))).reshape(B * S_pad, a.shape[-1])


def _group(x3, attend, state, W):
    B, S, _ = x3.shape
    T = B * S
    x = x3.reshape(T, D_MODEL)
    q, k, kb, v, vb, u, vm, om, ga, gb, g = _in_proj(x, W["w_main"], W["b_main"], W["w_gate"], W["b_gate"])
    oa = attend(q, k, v, kb, vb)

    c0, n0, m0, conv0 = state
    S_pad = -(-S // M_CHUNK) * M_CHUNK
    icol, fcol, irow, frow = _gate_layouts(g, B, S, S_pad)
    halo = jnp.pad(conv0, ((0, 0), (CONV_HALO - (M_CONV - 1), 0), (0, 0)))
    hb, c_new, n_new, m_new = _mlstm(
        _pad_seq(u, B, S, S_pad), halo, _pad_seq(vm, B, S, S_pad), _pad_seq(om, B, S, S_pad),
        icol, fcol, irow, frow, W["w_conv"], W["b_conv"], W["w_qm"], W["w_km"], W["w_kmt"], W["mnorm_g"],
        c0, n0.reshape(B, M_HEADS, 1, M_HEAD_DIM),
        jnp.broadcast_to(m0[:, :, None, None], (B, M_HEADS, 1, LANES)), B, S_pad)
    if S_pad != S:
        hb = hb.reshape(B, S_pad, D_MODEL)[:, :S].reshape(T, D_MODEL)
    u3 = u.reshape(B, S, D_MODEL)
    conv_new = jnp.concatenate([conv0, u3], axis=1)[:, S:]

    h1t, h1tb = _merge(x, oa, hb, ga, gb, W["w_a"], W["w_b"], W["w_o"], W["ln1_g"], W["ln1_b"])
    e1, lf, e2, r2 = _route(h1tb, W["w_pqt"], W["p_keys"])
    y = _peer(h1t, h1tb, e1, lf, e2, r2, W["p_u"], W["p_vt"], W["ln2_g"], W["ln2_b"])

    new_state = (k.reshape(1, B, S, A_HEADS, A_HD2), v.reshape(1, B, S, A_HEADS, A_HD2),
                 c_new[None], n_new.reshape(1, B, M_HEADS, M_HEAD_DIM), m_new[None, :, :, 0, 0],
                 conv_new[None])
    return y.reshape(B, S, D_MODEL), new_state


def kernel(x_prompt, x_sample, cache_k, cache_v, state_C, state_n, state_m, state_conv, page_table,
           w_in, b_in, lam_q1, lam_k1, lam_q2, lam_k2, subln_g, w_conv, b_conv, w_qm, w_km, mnorm_g,
           w_a, w_b, w_o, ln1_g, ln1_b, w_pq, p_keys, p_u, p_v, ln2_g, ln2_b):
    assert w_in.shape[0] == DEPTH
    B, S, _ = x_prompt.shape
    DB, DS, _ = x_sample.shape
    assert 2 * DS == SAMPLE_ROWS

    n_pre = 6 * D_MODEL
    wi, bi = w_in[0], b_in[0]
    W = {
        "w_main": jnp.concatenate([wi[:, :n_pre], wi[:, n_pre + GATE_COLS:]], axis=1).astype(BF16),
        "b_main": jnp.concatenate([bi[:n_pre], bi[n_pre + GATE_COLS:]])[None],
        "w_gate": jnp.pad(wi[:, n_pre:n_pre + GATE_COLS], ((0, 0), (0, LANES - GATE_COLS))).astype(BF16),
        "b_gate": jnp.pad(bi[n_pre:n_pre + GATE_COLS], (0, LANES - GATE_COLS))[None],
        "w_conv": w_conv[0], "b_conv": b_conv[0][None],
        "w_qm": w_qm[0].astype(BF16), "w_km": w_km[0].astype(BF16),
        "w_kmt": jnp.swapaxes(w_km[0], 1, 2).astype(BF16),
        "mnorm_g": mnorm_g[0][None],
        "w_a": w_a[0].astype(BF16), "w_b": w_b[0].astype(BF16), "w_o": w_o[0].astype(BF16),
        "ln1_g": ln1_g[0][None], "ln1_b": ln1_b[0][None],
        "w_pqt": w_pq[0].T.astype(BF16),
        "p_keys": p_keys[0].reshape(2 * P_HEADS, P_NKEYS, P_HALF_DIM).astype(BF16),
        "p_u": p_u[0].astype(BF16), "p_vt": p_v[0].T.astype(BF16),
        "ln2_g": ln2_g[0][None], "ln2_b": ln2_b[0][None],
    }
    lam_vecs = (lam_q1[0][None], lam_k1[0][None], lam_q2[0][None], lam_k2[0][None])
    sg = subln_g[0][None]

    def attend_prompt(q, k, v, kb, vb):
        return _attn_prompt(q, kb, vb.T, lam_vecs, subln_g[0][:, None], B, S)

    def attend_sample(q, k, v, kb, vb):
        q3 = q.reshape(DB, DS, D_MODEL)
        q8 = jnp.concatenate([q3, q3], axis=1)
        padn = ((0, 0), (0, SAMPLE_ROWS - DS), (0, 0), (0, 0))
        kn8 = jnp.pad(k.reshape(DB, DS, A_HEADS, A_HD2), padn)
        vn8 = jnp.pad(v.reshape(DB, DS, A_HEADS, A_HD2), padn)
        o8 = _attn_sample(q8, kn8, vn8, cache_k, cache_v, page_table, lam_vecs, sg)
        return o8[:, :DS].reshape(DB * DS, D_MODEL).astype(BF16)

    zero_state = (jnp.zeros((B, M_HEADS, M_HEAD_DIM, M_HEAD_DIM), F32),
                  jnp.zeros((B, M_HEADS, M_HEAD_DIM), F32),
                  jnp.zeros((B, M_HEADS), F32),
                  jnp.zeros((B, M_CONV - 1, D_MODEL), F32))
    yp, sp = _group(x_prompt, attend_prompt, zero_state, W)
    ys, ss = _group(x_sample, attend_sample,
                    (state_C[0], state_n[0], state_m[0], state_conv[0]), W)
    return (yp, ys) + sp + ss
```

```python
import functools
import math

import jax
import jax.numpy as jnp
from jax import lax
from jax.experimental import pallas as pl
from jax.experimental.pallas import tpu as pltpu

F32 = jnp.float32
BF16 = jnp.bfloat16

D_MODEL = 1024
A_HEADS = 8
A_HEAD_DIM = 64
A_HD2 = 2 * A_HEAD_DIM
M_HEADS = 4
M_HEAD_DIM = 256
M_CONV = 4
M_CHUNK = 64
P_HEADS = 8
P_NKEYS = 128
P_HALF_DIM = 128
P_TOPK = 16
P_EXPERTS = P_NKEYS * P_NKEYS
PAGE_SIZE = 128
DEPTH = 1
ALPHA = (2.0 * DEPTH) ** 0.25
LN_EPS = 1e-5
PAD_LOG_INPUT_GATE = -1e30
LAM_INIT = 0.8 - 0.6 * math.exp(-0.3 * 0)

N_MAIN_GROUPS = 8
GATE_COLS = 2 * M_HEADS
LANES = 128
VMEM_LIMIT = 56 * 1024 * 1024

NT_DIMS = (((1,), (1,)), ((), ()))


def _cparams(sem):
    return pltpu.CompilerParams(dimension_semantics=sem, vmem_limit_bytes=VMEM_LIMIT)


def _dot(a, b):
    return jnp.dot(a, b, preferred_element_type=F32)


def _dot_nt(a, b):
    return lax.dot_general(a, b, NT_DIMS, preferred_element_type=F32)


def _in_proj_kernel(x_ref, w_ref, b_ref, wg_ref, bg_ref,
                    q_ref, k_ref, kb_ref, v_ref, vb_ref, u_ref, vm_ref, om_ref, ga_ref, gb_ref,
                    g_ref):
    j = pl.program_id(1)

    def proj():
        return _dot(x_ref[...].astype(BF16), w_ref[...]) + b_ref[...]

    @pl.when(j == 0)
    def _():
        q_ref[...] = (proj() * (A_HEAD_DIM ** -0.5)).astype(BF16)
        g_ref[...] = _dot(x_ref[...].astype(BF16), wg_ref[...]) + bg_ref[...]

    @pl.when(j == 1)
    def _():
        k_ref[...] = proj()
        kb_ref[...] = k_ref[...].astype(BF16)

    @pl.when(j == 2)
    def _():
        v_ref[...] = proj()
        vb_ref[...] = v_ref[...].astype(BF16)

    for idx, ref in ((3, u_ref), (5, om_ref), (6, ga_ref), (7, gb_ref)):
        @pl.when(j == idx)
        def _(ref=ref):
            ref[...] = proj()

    @pl.when(j == 4)
    def _():
        vm_ref[...] = proj().astype(BF16)


def _in_proj(x, w_main, b_main, w_gate, b_gate):
    T = x.shape[0]
    tm = min(T, 512)
    assert T % tm == 0
    row = lambda i, j: (i, 0)
    f32o = jax.ShapeDtypeStruct((T, D_MODEL), F32)
    bf16o = jax.ShapeDtypeStruct((T, D_MODEL), BF16)
    ospec = pl.BlockSpec((tm, D_MODEL), row)
    return pl.pallas_call(
        _in_proj_kernel,
        grid=(T // tm, N_MAIN_GROUPS),
        in_specs=[
            pl.BlockSpec((tm, D_MODEL), row),
            pl.BlockSpec((D_MODEL, D_MODEL), lambda i, j: (0, j)),
            pl.BlockSpec((1, D_MODEL), lambda i, j: (0, j)),
            pl.BlockSpec((D_MODEL, LANES), lambda i, j: (0, 0)),
            pl.BlockSpec((1, LANES), lambda i, j: (0, 0)),
        ],
        out_specs=[ospec] * 10 + [pl.BlockSpec((tm, LANES), row)],
        out_shape=[bf16o, f32o, bf16o, f32o, bf16o, f32o, bf16o, f32o, f32o, f32o,
                   jax.ShapeDtypeStruct((T, LANES), F32)],
        compiler_params=_cparams(("arbitrary", "arbitrary")),
        name="in_proj",
    )(x, w_main, b_main, w_gate, b_gate)


def _lam_value(lq1, lk1, lq2, lk2):
    a = jnp.exp(jnp.sum(lq1 * lk1, axis=1, keepdims=True))
    b = jnp.exp(jnp.sum(lq2 * lk2, axis=1, keepdims=True))
    return a - b + LAM_INIT


def _sub_norm(o, g):
    r = lax.rsqrt(jnp.mean(o * o, axis=1, keepdims=True) + LN_EPS)
    return o * r * g * (1.0 - LAM_INIT)


ATTN_TQ = 256


def _attn_prompt_kernel(lq1_ref, lk1_ref, lq2_ref, lk2_ref, gcol_ref, q_ref, k_ref, vt_ref, o_ref, *, nq):
    i = pl.program_id(2)
    tq = ATTN_TQ
    lam = _lam_value(lq1_ref[...], lk1_ref[...], lq2_ref[...], lk2_ref[...])
    q = q_ref[...]
    lane = lax.broadcasted_iota(jnp.int32, q.shape, 1)
    zero = jnp.zeros_like(q)
    maps = (jnp.where(lane < A_HEAD_DIM, q, zero), jnp.where(lane >= A_HEAD_DIM, q, zero))
    krow = lax.broadcasted_iota(jnp.int32, (tq, tq), 0)
    qcol = lax.broadcasted_iota(jnp.int32, (tq, tq), 1)

    def scores(qm, n):
        lo, hi = n * tq, (n + 1) * tq
        sd = _dot_nt(k_ref[lo:hi, :], qm)
        st = _dot_nt(k_ref[0:lo, :], qm) if n else None
        return sd, st

    def attend(sd, st, n):
        lo, hi = n * tq, (n + 1) * tq
        sd = jnp.where(krow <= qcol, sd, -jnp.inf)
        m = jnp.max(sd, axis=0, keepdims=True)
        if n:
            m = jnp.maximum(m, jnp.max(st, axis=0, keepdims=True))
        pd = jnp.exp(sd - m)
        l = jnp.sum(pd, axis=0, keepdims=True)
        acc = _dot(vt_ref[:, lo:hi], pd.astype(BF16))
        if n:
            pt = jnp.exp(st - m)
            l = l + jnp.sum(pt, axis=0, keepdims=True)
            acc = acc + _dot(vt_ref[:, 0:lo], pt.astype(BF16))
        return acc / l

    for n in range(nq):
        @pl.when(i == n)
        def _(n=n):
            s1, s2 = scores(maps[0], n), scores(maps[1], n)
            o = attend(*s1, n) - lam * attend(*s2, n)
            r = lax.rsqrt(jnp.mean(o * o, axis=0, keepdims=True) + LN_EPS)
            o = o * r * gcol_ref[...] * (1.0 - LAM_INIT)
            o_ref[...] = o.T.astype(BF16)


def _attn_prompt(q, kb, vbt, lam_vecs, subln_gcol, B, S):
    tq = ATTN_TQ
    assert S % tq == 0
    nq = S // tq
    vec = pl.BlockSpec((1, A_HEAD_DIM), lambda b, h, i: (0, 0))
    return pl.pallas_call(
        functools.partial(_attn_prompt_kernel, nq=nq),
        grid=(B, A_HEADS, nq),
        in_specs=[vec, vec, vec, vec,
                  pl.BlockSpec((A_HD2, 1), lambda b, h, i: (0, 0)),
                  pl.BlockSpec((tq, A_HD2), lambda b, h, i: (b * nq + i, h)),
                  pl.BlockSpec((S, A_HD2), lambda b, h, i: (b, h)),
                  pl.BlockSpec((A_HD2, S), lambda b, h, i: (h, b))],
        out_specs=pl.BlockSpec((tq, A_HD2), lambda b, h, i: (b * nq + i, h)),
        out_shape=jax.ShapeDtypeStruct((B * S, D_MODEL), BF16),
        compiler_params=_cparams(("arbitrary", "arbitrary", "arbitrary")),
        name="attn_prompt",
    )(*lam_vecs, subln_gcol, q, kb, vbt)


SAMPLE_ROWS = 8
PAGES_PER_STEP = 8


def _attn_sample_kernel(pt_ref, lq1_ref, lk1_ref, lq2_ref, lk2_ref, g_ref, q_ref, kn_ref, vn_ref, *rest,
                        n_steps, dec_seq):
    pp = PAGES_PER_STEP
    k_refs, v_refs = rest[:pp], rest[pp:2 * pp]
    o_ref, qf_ref, bias_ref, m_ref, l_ref, acc_ref = rest[2 * pp:]
    step_id = pl.program_id(1)
    rows = A_HEADS * SAMPLE_ROWS
    flat = PAGE_SIZE * A_HEADS

    @pl.when(step_id == 0)
    def _():
        q8 = q_ref[0]
        qf = jnp.concatenate([q8[:, h * A_HD2:(h + 1) * A_HD2] for h in range(A_HEADS)], axis=0)
        row = lax.broadcasted_iota(jnp.int32, qf.shape, 0)
        lane = lax.broadcasted_iota(jnp.int32, qf.shape, 1)
        first_map = (row % SAMPLE_ROWS) < dec_seq
        qf_ref[...] = jnp.where(first_map == (lane < A_HEAD_DIM), qf, jnp.zeros_like(qf))
        brow = lax.broadcasted_iota(jnp.int32, (rows, flat), 0)
        bcol = lax.broadcasted_iota(jnp.int32, (rows, flat), 1)
        bias_ref[...] = jnp.where((bcol % A_HEADS) == (brow // SAMPLE_ROWS), 0.0, -jnp.inf)
        m_ref[...] = jnp.full(m_ref.shape, -jnp.inf, F32)
        l_ref[...] = jnp.zeros(l_ref.shape, F32)
        acc_ref[...] = jnp.zeros(acc_ref.shape, F32)

    def update(kfs, vfs, bias):
        ss = [_dot_nt(qf_ref[...], kf) + bias for kf in kfs]
        m = m_ref[...]
        m_new = m
        for s in ss:
            m_new = jnp.maximum(m_new, jnp.max(s, axis=1, keepdims=True))
        alpha = jnp.exp(m - m_new)
        l = alpha * l_ref[...]
        acc = alpha * acc_ref[...]
        for s, vf in zip(ss, vfs):
            p = jnp.exp(s - m_new)
            l = l + jnp.sum(p, axis=1, keepdims=True)
            acc = acc + _dot(p.astype(BF16), vf)
        l_ref[...] = l
        acc_ref[...] = acc
        m_ref[...] = m_new

    update([k_refs[t][0, 0].reshape(flat, A_HD2).astype(BF16) for t in range(pp)],
           [v_refs[t][0, 0].reshape(flat, A_HD2).astype(BF16) for t in range(pp)], bias_ref[...])

    @pl.when(step_id == n_steps - 1)
    def _():
        nnew = SAMPLE_ROWS * A_HEADS
        pad = jnp.zeros((PAGE_SIZE - nnew, A_HD2), BF16)
        kf = jnp.concatenate([kn_ref[0].reshape(nnew, A_HD2).astype(BF16), pad], axis=0)
        vf = jnp.concatenate([vn_ref[0].reshape(nnew, A_HD2).astype(BF16), pad], axis=0)
        row = lax.broadcasted_iota(jnp.int32, (rows, PAGE_SIZE), 0)
        col = lax.broadcasted_iota(jnp.int32, (rows, PAGE_SIZE), 1)
        ok = ((col % A_HEADS) == (row // SAMPLE_ROWS)) & ((col // A_HEADS) <= (row % dec_seq))
        update([kf], [vf], jnp.where(ok, 0.0, -jnp.inf))
        lam = _lam_value(lq1_ref[...], lk1_ref[...], lq2_ref[...], lk2_ref[...])
        g = g_ref[...]
        for h in range(A_HEADS):
            rs = slice(h * SAMPLE_ROWS, (h + 1) * SAMPLE_ROWS)
            o = acc_ref[rs, :] / l_ref[rs, :]
            o = o - lam * pltpu.roll(o, SAMPLE_ROWS - dec_seq, 0)
            o_ref[0, :, h * A_HD2:(h + 1) * A_HD2] = _sub_norm(o, g)


def _attn_sample(q8, kn8, vn8, cache_k, cache_v, page_table, lam_vecs, subln_g):
    DB, n_pages = page_table.shape
    pp = PAGES_PER_STEP
    assert n_pages % pp == 0
    n_steps = n_pages // pp
    dec_seq = SAMPLE_ROWS // 2
    rows = A_HEADS * SAMPLE_ROWS
    vec = pl.BlockSpec((1, A_HEAD_DIM), lambda b, s, pt: (0, 0))
    per_b = pl.BlockSpec((1, SAMPLE_ROWS, D_MODEL), lambda b, s, pt: (b, 0, 0))
    new_kv = pl.BlockSpec((1, SAMPLE_ROWS, A_HEADS, A_HD2), lambda b, s, pt: (b, 0, 0, 0))

    def page_spec(t):
        return pl.BlockSpec((1, 1, PAGE_SIZE, A_HEADS, A_HD2),
                            lambda b, s, pt: (0, pt[b, s * pp + t], 0, 0, 0))

    grid_spec = pltpu.PrefetchScalarGridSpec(
        num_scalar_prefetch=1,
        grid=(DB, n_steps),
        in_specs=[vec, vec, vec, vec, pl.BlockSpec((1, A_HD2), lambda b, s, pt: (0, 0)),
                  per_b, new_kv, new_kv]
                 + [page_spec(t) for t in range(pp)] + [page_spec(t) for t in range(pp)],
        out_specs=per_b,
        scratch_shapes=[pltpu.VMEM((rows, A_HD2), BF16),
                        pltpu.VMEM((rows, PAGE_SIZE * A_HEADS), F32),
                        pltpu.VMEM((rows, 1), F32), pltpu.VMEM((rows, 1), F32),
                        pltpu.VMEM((rows, A_HD2), F32)],
    )
    return pl.pallas_call(
        functools.partial(_attn_sample_kernel, n_steps=n_steps, dec_seq=dec_seq),
        grid_spec=grid_spec,
        out_shape=jax.ShapeDtypeStruct((DB, SAMPLE_ROWS, D_MODEL), F32),
        compiler_params=_cparams(("arbitrary", "arbitrary")),
        name="attn_sample",
    )(page_table, *lam_vecs, subln_g, q8, kn8, vn8, *([cache_k] * pp), *([cache_v] * pp))


CONV_HALO = 8
MLSTM_HEADS_PER_STEP = 2


def _log_sigmoid(x):
    return -(jnp.maximum(-x, 0.0) + jnp.log1p(jnp.exp(-jnp.abs(x))))


def _mlstm_kernel(u_ref, halo_ref, vm_ref, om_ref, icol_ref, fcol_ref, irow_ref, frow_ref,
                  wconv_ref, bconv_ref, wq_ref, wk_ref, wkt_ref, g_ref, c0_ref, n0_ref, m0_ref,
                  hb_ref, c_ref, n_ref, m_ref,
                  ext_ref, uc_ref, q_ref, k_ref, ct_ref, nrow_ref, *, S, L):
    kscale = M_HEAD_DIM ** -0.5
    dh = M_HEAD_DIM
    heads = range(MLSTM_HEADS_PER_STEP)
    hcols = [slice(hh * dh, (hh + 1) * dh) for hh in heads]
    ext_ref[0:CONV_HALO, :] = halo_ref[0]
    ext_ref[CONV_HALO:CONV_HALO + S, :] = u_ref[...]
    rb = min(S, 512)
    for r0 in range(0, S, rb):
        acc = jnp.broadcast_to(bconv_ref[...], (rb, ext_ref.shape[1]))
        for jj in range(M_CONV):
            off = CONV_HALO - (M_CONV - 1) + jj + r0
            acc = acc + ext_ref[off:off + rb, :] * wconv_ref[jj:jj + 1, :]
        uc = (acc * jax.nn.sigmoid(acc)).astype(BF16)
        uc_ref[r0:r0 + rb, :] = uc
        for hh in heads:
            q_ref[r0:r0 + rb, hcols[hh]] = _dot(uc[:, hcols[hh]], wq_ref[hh]).astype(BF16)
            k_ref[r0:r0 + rb, hcols[hh]] = (_dot(uc[:, hcols[hh]], wk_ref[hh]) * kscale).astype(BF16)

    for hh in heads:
        ct_ref[hh] = c0_ref[0, hh].T
        nrow_ref[hh] = n0_ref[0, hh]
    ti = lax.broadcasted_iota(jnp.int32, (L, L), 0)
    si = lax.broadcasted_iota(jnp.int32, (L, L), 1)
    tril = si <= ti
    gain = g_ref[...]

    def chunk(c, m_prevs):
        r0 = pl.multiple_of(c * L, L)
        rows = pl.ds(r0, L)
        qcs = [q_ref[rows, hcols[hh]] for hh in heads]
        vcs = [vm_ref[rows, hcols[hh]] for hh in heads]
        kcts = [(_dot_nt(wkt_ref[hh], uc_ref[rows, hcols[hh]]) * kscale).astype(BF16) for hh in heads]
        sqk = [_dot(qcs[hh], kcts[hh]) for hh in heads]
        qct = [_dot(qcs[hh], ct_ref[hh].astype(BF16)) for hh in heads]
        gm = []
        for hh in heads:
            m_prev = m_prevs[hh]
            icol = icol_ref[0, hh, rows, :]
            fcol = _log_sigmoid(fcol_ref[0, hh, rows, :])
            irow = irow_ref[0, hh, pl.ds(c, 1), :]
            frow = _log_sigmoid(frow_ref[0, hh, pl.ds(c, 1), :])
            b_col = jnp.sum(jnp.where(tril, frow, 0.0), axis=1, keepdims=True)
            b_row = jnp.sum(jnp.where(ti <= si, fcol, 0.0), axis=0, keepdims=True)
            dmat = jnp.where(tril, b_col - b_row + irow, -jnp.inf)
            m_t = jnp.maximum(b_col + m_prev, jnp.max(dmat, axis=1, keepdims=True))
            wmat = jnp.exp(dmat - m_t)
            inter = jnp.exp(b_col + m_prev - m_t)
            m_new = m_t[L - 1:L, :]
            b_last = b_col[L - 1:L, :]
            decay = jnp.exp(b_last + m_prev - m_new)
            w_col = jnp.exp(b_last - b_col + icol - m_new)
            gm.append((m_t, wmat, inter, m_new, decay, w_col))
        sws = [sqk[hh] * gm[hh][1] for hh in heads]
        wvs = [(gm[hh][5] * vcs[hh].astype(F32)).astype(BF16) for hh in heads]
        swv = [_dot(sws[hh].astype(BF16), vcs[hh]) for hh in heads]
        upd = [_dot(kcts[hh], wvs[hh]) for hh in heads]
        m_news = []
        for hh in heads:
            m_t, wmat, inter, m_new, decay, w_col = gm[hh]
            qc = qcs[hh]
            kc = k_ref[rows, hcols[hh]]
            num = inter * qct[hh] + swv[hh]
            qn = jnp.sum(qc.astype(F32) * nrow_ref[hh], axis=1, keepdims=True)
            dot = inter * qn + jnp.sum(sws[hh], axis=1, keepdims=True)
            h = num / jnp.maximum(jnp.abs(dot), jnp.exp(-m_t))
            ct_ref[hh] = decay * ct_ref[hh] + upd[hh]
            nrow_ref[hh] = decay * nrow_ref[hh] + jnp.sum(w_col * kc.astype(F32), axis=0, keepdims=True)
            mu = jnp.mean(h, axis=1, keepdims=True)
            hc = h - mu
            var = jnp.mean(hc * hc, axis=1, keepdims=True)
            hn = hc * lax.rsqrt(var + LN_EPS) * gain[:, hcols[hh]]
            hb_ref[rows, hcols[hh]] = (jax.nn.sigmoid(om_ref[rows, hcols[hh]]) * hn).astype(BF16)
            m_news.append(m_new)
        return tuple(m_news)

    m_fin = lax.fori_loop(0, S // L, chunk, tuple(m0_ref[0, hh][:, 0:1] for hh in heads))
    for hh in heads:
        c_ref[0, hh] = ct_ref[hh].T
        n_ref[0, hh] = nrow_ref[hh]
        m_ref[0, hh] = jnp.broadcast_to(m_fin[hh], (1, LANES))


def _mlstm(u, halo, vm, om, icol, fcol, irow, frow, w_conv, b_conv, wq, wk, wkt, mnorm_g,
           c0, n0, m0, B, S):
    L = M_CHUNK
    nc = S // L
    H, dh = M_HEADS, M_HEAD_DIM
    hp = MLSTM_HEADS_PER_STEP
    assert H % hp == 0
    seq = pl.BlockSpec((S, hp * dh), lambda b, h: (b, h))
    col = pl.BlockSpec((1, hp, S, 1), lambda b, h: (b, h, 0, 0))
    rowg = pl.BlockSpec((1, hp, nc, L), lambda b, h: (b, h, 0, 0))
    wmat = pl.BlockSpec((hp, dh, dh), lambda b, h: (h, 0, 0))
    state = pl.BlockSpec((1, hp, dh, dh), lambda b, h: (b, h, 0, 0))
    nvec = pl.BlockSpec((1, hp, 1, dh), lambda b, h: (b, h, 0, 0))
    mvec = pl.BlockSpec((1, hp, 1, LANES), lambda b, h: (b, h, 0, 0))
    return pl.pallas_call(
        functools.partial(_mlstm_kernel, S=S, L=L),
        grid=(B, H // hp),
        in_specs=[seq,
                  pl.BlockSpec((1, CONV_HALO, hp * dh), lambda b, h: (b, 0, h)),
                  seq, seq, col, col, rowg, rowg,
                  pl.BlockSpec((M_CONV, hp * dh), lambda b, h: (0, h)),
                  pl.BlockSpec((1, hp * dh), lambda b, h: (0, h)),
                  wmat, wmat, wmat,
                  pl.BlockSpec((1, hp * dh), lambda b, h: (0, h)),
                  state, nvec, mvec],
        out_specs=[seq, state, nvec, mvec],
        out_shape=[jax.ShapeDtypeStruct((B * S, D_MODEL), BF16),
                   jax.ShapeDtypeStruct((B, H, dh, dh), F32),
                   jax.ShapeDtypeStruct((B, H, 1, dh), F32),
                   jax.ShapeDtypeStruct((B, H, 1, LANES), F32)],
        scratch_shapes=[pltpu.VMEM((S + CONV_HALO, hp * dh), F32),
                        pltpu.VMEM((S, hp * dh), BF16), pltpu.VMEM((S, hp * dh), BF16),
                        pltpu.VMEM((S, hp * dh), BF16),
                        pltpu.VMEM((hp, dh, dh), F32), pltpu.VMEM((hp, 1, dh), F32)],
        compiler_params=_cparams(("arbitrary", "arbitrary")),
        name="mlstm",
    )(u, halo, vm, om, icol, fcol, irow, frow, w_conv, b_conv, wq, wk, wkt, mnorm_g, c0, n0, m0)


def _layer_norm_rows(z, g, b):
    mu = jnp.mean(z, axis=1, keepdims=True)
    zc = z - mu
    var = jnp.mean(zc * zc, axis=1, keepdims=True)
    return zc * lax.rsqrt(var + LN_EPS) * g + b


def _merge_kernel(x_ref, oa_ref, hb_ref, ga_ref, gb_ref, wa_ref, wb_ref, wo_ref, g_ref, b_ref,
                  h1t_ref, h1tb_ref):
    ya = _dot(oa_ref[...], wa_ref[...])
    yb = _dot(hb_ref[...], wb_ref[...])
    merged = jax.nn.sigmoid(ga_ref[...]) * ya + jax.nn.sigmoid(gb_ref[...]) * yb
    out = _dot(merged.astype(BF16), wo_ref[...])
    h1 = _layer_norm_rows(ALPHA * x_ref[...] + out, g_ref[...], b_ref[...])
    h1t = h1.T
    h1t_ref[...] = h1t
    h1tb_ref[...] = h1t.astype(BF16)


def _merge(x, oa, hb, ga, gb, wa, wb, wo, ln_g, ln_b):
    T = x.shape[0]
    tm = min(T, 512)
    tok = pl.BlockSpec((tm, D_MODEL), lambda i: (i, 0))
    wsp = pl.BlockSpec((D_MODEL, D_MODEL), lambda i: (0, 0))
    vsp = pl.BlockSpec((1, D_MODEL), lambda i: (0, 0))
    tsp = pl.BlockSpec((D_MODEL, tm), lambda i: (0, i))
    return pl.pallas_call(
        _merge_kernel,
        grid=(T // tm,),
        in_specs=[tok, tok, tok, tok, tok, wsp, wsp, wsp, vsp, vsp],
        out_specs=[tsp, tsp],
        out_shape=[jax.ShapeDtypeStruct((D_MODEL, T), F32), jax.ShapeDtypeStruct((D_MODEL, T), BF16)],
        compiler_params=_cparams(("arbitrary",)),
        name="merge_ln1",
    )(x, oa, hb, ga, gb, wa, wb, wo, ln_g, ln_b)


NOT_SELECTED = 255.0


def _top16_ranks(s, iota):
    rank = jnp.full(s.shape, NOT_SELECTED, F32)
    tops = []
    for r in range(P_TOPK):
        mx = jnp.max(s, axis=0, keepdims=True)
        idx = jnp.min(jnp.where(s == mx, iota, float(P_NKEYS)), axis=0, keepdims=True)
        sel = iota == idx
        rank = jnp.where(sel, float(r), rank)
        s = jnp.where(sel, -jnp.inf, s)
        tops.append(mx)
    return rank, jnp.concatenate(tops, axis=0)


def _top16(s, iota):
    s0 = s
    rank = jnp.full(s.shape, NOT_SELECTED, F32)
    tops = []
    for r in range(P_TOPK):
        mx = jnp.max(s, axis=0, keepdims=True)
        eq = s == mx
        rank = jnp.where(eq, float(r), rank)
        s = jnp.where(eq, -jnp.inf, s)
        tops.append(mx)
    tops = jnp.concatenate(tops, axis=0)
    n_ranked = jnp.sum(jnp.where(rank < float(P_TOPK), 1.0, 0.0), axis=0, keepdims=True)
    tied = jnp.max(n_ranked) > float(P_TOPK)
    return lax.cond(tied, lambda: _top16_ranks(s0, iota), lambda: (rank, tops))


def _pair_merge(sv1, sv2, riota):
    cnt = jnp.zeros(sv1.shape, F32)
    front = sv1 + sv2[0:1, :]
    top0 = front[0:1, :]
    z = jnp.zeros_like(top0)
    for _ in range(P_TOPK):
        fm = jnp.max(front, axis=0, keepdims=True)
        idx = jnp.min(jnp.where(front == fm, riota, float(P_TOPK)), axis=0, keepdims=True)
        sel = riota == idx
        z = z + jnp.exp(fm - top0)
        cnt = cnt + jnp.where(sel, 1.0, 0.0)
        csel = jnp.sum(jnp.where(sel, cnt, 0.0), axis=0, keepdims=True)
        v1 = jnp.sum(jnp.where(sel, sv1, 0.0), axis=0, keepdims=True)
        v2 = jnp.sum(jnp.where(riota == csel, sv2, 0.0), axis=0, keepdims=True)
        nxt = jnp.where(csel < float(P_TOPK), v1 + v2, -jnp.inf)
        front = jnp.where(sel, nxt, front)
    return cnt, z


def _route_kernel(h1t_ref, wpqt_ref, keys_ref, e1_ref, lf_ref, e2_ref, r2_ref, qt_ref, *, tt):
    qt_ref[...] = _dot(wpqt_ref[...], h1t_ref[...]).astype(BF16)
    kiota = lax.broadcasted_iota(jnp.int32, (P_NKEYS, LANES), 0).astype(F32)
    riota = lax.broadcasted_iota(jnp.int32, (P_TOPK, LANES), 0).astype(F32)

    def head(h, carry):
        for w0 in range(0, tt, LANES):
            ws = slice(w0, w0 + LANES)
            r1 = pl.multiple_of(h * 2 * P_HALF_DIM, P_HALF_DIM)
            r2 = pl.multiple_of(h * 2 * P_HALF_DIM + P_HALF_DIM, P_HALF_DIM)
            s1 = _dot(keys_ref[2 * h], qt_ref[pl.ds(r1, P_HALF_DIM), ws])
            s2 = _dot(keys_ref[2 * h + 1], qt_ref[pl.ds(r2, P_HALF_DIM), ws])
            rank1, sv1 = _top16(s1, kiota)
            rank2, sv2 = _top16(s2, kiota)
            cnt, z = _pair_merge(sv1, sv2, riota)
            in1 = rank1 < float(P_TOPK)
            in2 = rank2 < float(P_TOPK)
            e1 = jnp.where(in1, jnp.exp(s1 - sv1[0:1, :]) / z, 0.0)
            e2 = jnp.where(in2, jnp.exp(s2 - sv2[0:1, :]), 0.0)
            lf = jnp.zeros_like(s1)
            for r in range(P_TOPK):
                lf = jnp.where(rank1 == float(r), cnt[r:r + 1, :], lf)
            e1_ref[h, :, ws] = e1
            lf_ref[h, :, ws] = lf
            e2_ref[h, :, ws] = e2.astype(BF16)
            r2_ref[h, :, ws] = rank2.astype(BF16)
        return carry

    lax.fori_loop(0, P_HEADS, head, 0)


def _route(h1tb, wpqt, keys):
    T = h1tb.shape[1]
    tt = min(T, 512)
    osp = pl.BlockSpec((P_HEADS, P_NKEYS, tt), lambda i: (0, 0, i))
    of32 = jax.ShapeDtypeStruct((P_HEADS, P_NKEYS, T), F32)
    obf = jax.ShapeDtypeStruct((P_HEADS, P_NKEYS, T), BF16)
    return pl.pallas_call(
        functools.partial(_route_kernel, tt=tt),
        grid=(T // tt,),
        in_specs=[pl.BlockSpec((D_MODEL, tt), lambda i: (0, i)),
                  pl.BlockSpec((2 * P_HEADS * P_HALF_DIM, D_MODEL), lambda i: (0, 0)),
                  pl.BlockSpec((2 * P_HEADS, P_NKEYS, P_HALF_DIM), lambda i: (0, 0, 0))],
        out_specs=[osp, osp, osp, osp],
        out_shape=[of32, of32, obf, obf],
        scratch_shapes=[pltpu.VMEM((2 * P_HEADS * P_HALF_DIM, tt), BF16)],
        compiler_params=_cparams(("arbitrary",)),
        name="peer_route",
    )(h1tb, wpqt, keys)


PEER_EXPERT_BLOCK = 1024
PEER_SUB_BLOCK = 256


def _peer_kernel(h1t_ref, h1tb_ref, e1_ref, lf_ref, e2_ref, r2_ref, pu_ref, pvt_ref, g_ref, b_ref,
                 y_ref, acc_ref, *, n_eb):
    eb = pl.program_id(1)
    rows_per_block = PEER_EXPERT_BLOCK // P_NKEYS

    @pl.when(eb == 0)
    def _():
        acc_ref[...] = jnp.zeros(acc_ref.shape, F32)

    sub = PEER_SUB_BLOCK
    n_sub = PEER_EXPERT_BLOCK // sub

    def scores(sb):
        return _dot(pu_ref[sb * sub:(sb + 1) * sub, :], h1tb_ref[...])

    a_next = scores(0)
    for sb in range(n_sub):
        a = a_next
        if sb + 1 < n_sub:
            a_next = scores(sb + 1)
        act = (0.5 * a * (1.0 + lax.erf(a * (2.0 ** -0.5)))).astype(BF16)
        gates = []
        for ci in range(sub // P_NKEYS):
            c = eb * rows_per_block + sb * (sub // P_NKEYS) + ci
            gate = jnp.zeros((P_NKEYS, act.shape[1]), BF16)
            for h in range(P_HEADS):
                lrow = lf_ref[h, pl.ds(c, 1), :].astype(BF16)
                erow = e1_ref[h, pl.ds(c, 1), :].astype(BF16)
                picked = jnp.where(r2_ref[h] < lrow, e2_ref[h], jnp.zeros_like(gate))
                gate = gate + picked * erow
            gates.append(gate)
        w = jnp.concatenate(gates, axis=0) * act
        acc_ref[...] += _dot(pvt_ref[:, sb * sub:(sb + 1) * sub], w)

    @pl.when(eb == n_eb - 1)
    def _():
        z = (ALPHA * h1t_ref[...] + acc_ref[...]).T
        y_ref[...] = _layer_norm_rows(z, g_ref[...], b_ref[...])


def _peer(h1t, h1tb, e1, lf, e2, r2, pu, pvt, ln_g, ln_b):
    T = h1t.shape[1]
    tt = min(T, 512)
    n_eb = P_EXPERTS // PEER_EXPERT_BLOCK
    tsp = pl.BlockSpec((D_MODEL, tt), lambda i, e: (0, i))
    rsp = pl.BlockSpec((P_HEADS, P_NKEYS, tt), lambda i, e: (0, 0, i))
    vsp = pl.BlockSpec((1, D_MODEL), lambda i, e: (0, 0))
    return pl.pallas_call(
        functools.partial(_peer_kernel, n_eb=n_eb),
        grid=(T // tt, n_eb),
        in_specs=[tsp, tsp, rsp, rsp, rsp, rsp,
                  pl.BlockSpec((PEER_EXPERT_BLOCK, D_MODEL), lambda i, e: (e, 0)),
                  pl.BlockSpec((D_MODEL, PEER_EXPERT_BLOCK), lambda i, e: (0, e)),
                  vsp, vsp],
        out_specs=pl.BlockSpec((tt, D_MODEL), lambda i, e: (i, 0)),
        out_shape=jax.ShapeDtypeStruct((T, D_MODEL), F32),
        scratch_shapes=[pltpu.VMEM((D_MODEL, tt), F32)],
        compiler_params=_cparams(("arbitrary", "arbitrary")),
        name="peer_dense",
    )(h1t, h1tb, e1, lf, e2, r2, pu, pvt, ln_g, ln_b)


def _gate_layouts(g, B, S_real, S_pad):
    H, L = M_HEADS, M_CHUNK
    gi = g[:, :H].reshape(B, S_real, H)
    gf = g[:, H:2 * H].reshape(B, S_real, H)
    if S_pad != S_real:
        pw = ((0, 0), (0, S_pad - S_real), (0, 0))
        gi = jnp.pad(gi, pw, constant_values=PAD_LOG_INPUT_GATE)
        gf = jnp.pad(gf, pw, constant_values=-PAD_LOG_INPUT_GATE)
    gi = jnp.swapaxes(gi, 1, 2)
    gf = jnp.swapaxes(gf, 1, 2)
    col = lambda a: a.reshape(B, H, S_pad, 1)
    row = lambda a: a.reshape(B, H, S_pad // L, L)
    return col(gi), col(gf), row(gi), row(gf)


def _pad_seq(a, B, S_real, S_pad):
    if S_pad == S_real:
        return a
    a = a.reshape(B, S_real, a.shape[-1])
    return jnp.pad(a, ((0, 0), (0, S_pad - S_real), (0, 0))).reshape(B * S_pad, a.shape[-1])


def _group(x3, attend, state, W):
    B, S, _ = x3.shape
    T = B * S
    x = x3.reshape(T, D_MODEL)
    q, k, kb, v, vb, u, vm, om, ga, gb, g = _in_proj(x, W["w_main"], W["b_main"], W["w_gate"], W["b_gate"])
    oa = attend(q, k, v, kb, vb)

    c0, n0, m0, conv0 = state
    S_pad = -(-S // M_CHUNK) * M_CHUNK
    icol, fcol, irow, frow = _gate_layouts(g, B, S, S_pad)
    halo = jnp.pad(conv0, ((0, 0), (CONV_HALO - (M_CONV - 1), 0), (0, 0)))
    hb, c_new, n_new, m_new = _mlstm(
        _pad_seq(u, B, S, S_pad), halo, _pad_seq(vm, B, S, S_pad), _pad_seq(om, B, S, S_pad),
        icol, fcol, irow, frow, W["w_conv"], W["b_conv"], W["w_qm"], W["w_km"], W["w_kmt"], W["mnorm_g"],
        c0, n0.reshape(B, M_HEADS, 1, M_HEAD_DIM),
        jnp.broadcast_to(m0[:, :, None, None], (B, M_HEADS, 1, LANES)), B, S_pad)
    if S_pad != S:
        hb = hb.reshape(B, S_pad, D_MODEL)[:, :S].reshape(T, D_MODEL)
    u3 = u.reshape(B, S, D_MODEL)
    conv_new = jnp.concatenate([conv0, u3], axis=1)[:, S:]

    h1t, h1tb = _merge(x, oa, hb, ga, gb, W["w_a"], W["w_b"], W["w_o"], W["ln1_g"], W["ln1_b"])
    e1, lf, e2, r2 = _route(h1tb, W["w_pqt"], W["p_keys"])
    y = _peer(h1t, h1tb, e1, lf, e2, r2, W["p_u"], W["p_vt"], W["ln2_g"], W["ln2_b"])

    new_state = (k.reshape(1, B, S, A_HEADS, A_HD2), v.reshape(1, B, S, A_HEADS, A_HD2),
                 c_new[None], n_new.reshape(1, B, M_HEADS, M_HEAD_DIM), m_new[None, :, :, 0, 0],
                 conv_new[None])
    return y.reshape(B, S, D_MODEL), new_state


def kernel(x_prompt, x_sample, cache_k, cache_v, state_C, state_n, state_m, state_conv, page_table,
           w_in, b_in, lam_q1, lam_k1, lam_q2, lam_k2, subln_g, w_conv, b_conv, w_qm, w_km, mnorm_g,
           w_a, w_b, w_o, ln1_g, ln1_b, w_pq, p_keys, p_u, p_v, ln2_g, ln2_b):
    assert w_in.shape[0] == DEPTH
    B, S, _ = x_prompt.shape
    DB, DS, _ = x_sample.shape
    assert 2 * DS == SAMPLE_ROWS

    n_pre = 6 * D_MODEL
    wi, bi = w_in[0], b_in[0]
    W = {
        "w_main": jnp.concatenate([wi[:, :n_pre], wi[:, n_pre + GATE_COLS:]], axis=1).astype(BF16),
        "b_main": jnp.concatenate([bi[:n_pre], bi[n_pre + GATE_COLS:]])[None],
        "w_gate": jnp.pad(wi[:, n_pre:n_pre + GATE_COLS], ((0, 0), (0, LANES - GATE_COLS))).astype(BF16),
        "b_gate": jnp.pad(bi[n_pre:n_pre + GATE_COLS], (0, LANES - GATE_COLS))[None],
        "w_conv": w_conv[0], "b_conv": b_conv[0][None],
        "w_qm": w_qm[0].astype(BF16), "w_km": w_km[0].astype(BF16),
        "w_kmt": jnp.swapaxes(w_km[0], 1, 2).astype(BF16),
        "mnorm_g": mnorm_g[0][None],
        "w_a": w_a[0].astype(BF16), "w_b": w_b[0].astype(BF16), "w_o": w_o[0].astype(BF16),
        "ln1_g": ln1_g[0][None], "ln1_b": ln1_b[0][None],
        "w_pqt": w_pq[0].T.astype(BF16),
        "p_keys": p_keys[0].reshape(2 * P_HEADS, P_NKEYS, P_HALF_DIM).astype(BF16),
        "p_u": p_u[0].astype(BF16), "p_vt": p_v[0].T.astype(BF16),
        "ln2_g": ln2_g[0][None], "ln2_b": ln2_b[0][None],
    }
    lam_vecs = (lam_q1[0][None], lam_k1[0][None], lam_q2[0][None], lam_k2[0][None])
    sg = subln_g[0][None]

    def attend_prompt(q, k, v, kb, vb):
        return _attn_prompt(q, kb, vb.T, lam_vecs, subln_g[0][:, None], B, S)

    def attend_sample(q, k, v, kb, vb):
        q3 = q.reshape(DB, DS, D_MODEL)
        q8 = jnp.concatenate([q3, q3], axis=1)
        padn = ((0, 0), (0, SAMPLE_ROWS - DS), (0, 0), (0, 0))
        kn8 = jnp.pad(k.reshape(DB, DS, A_HEADS, A_HD2), padn)
        vn8 = jnp.pad(v.reshape(DB, DS, A_HEADS, A_HD2), padn)
        o8 = _attn_sample(q8, kn8, vn8, cache_k, cache_v, page_table, lam_vecs, sg)
        return o8[:, :DS].reshape(DB * DS, D_MODEL).astype(BF16)

    zero_state = (jnp.zeros((B, M_HEADS, M_HEAD_DIM, M_HEAD_DIM), F32),
                  jnp.zeros((B, M_HEADS, M_HEAD_DIM), F32),
                  jnp.zeros((B, M_HEADS), F32),
                  jnp.zeros((B, M_CONV - 1, D_MODEL), F32))
    yp, sp = _group(x_prompt, attend_prompt, zero_state, W)
    ys, ss = _group(x_sample, attend_sample,
                    (state_C[0], state_n[0], state_m[0], state_conv[0]), W)
    return (yp, ys) + sp + ss
```

```python
import functools
import math

import jax
import jax.numpy as jnp
from jax import lax
from jax.experimental import pallas as pl
from jax.experimental.pallas import tpu as pltpu

F32 = jnp.float32
BF16 = jnp.bfloat16

D_MODEL = 1024
A_HEADS = 8
A_HEAD_DIM = 64
A_HD2 = 2 * A_HEAD_DIM
M_HEADS = 4
M_HEAD_DIM = 256
M_CONV = 4
M_CHUNK = 64
P_HEADS = 8
P_NKEYS = 128
P_HALF_DIM = 128
P_TOPK = 16
P_EXPERTS = P_NKEYS * P_NKEYS
PAGE_SIZE = 128
DEPTH = 1
ALPHA = (2.0 * DEPTH) ** 0.25
LN_EPS = 1e-5
PAD_LOG_INPUT_GATE = -1e30
LAM_INIT = 0.8 - 0.6 * math.exp(-0.3 * 0)

N_MAIN_GROUPS = 8
GATE_COLS = 2 * M_HEADS
LANES = 128
VMEM_LIMIT = 56 * 1024 * 1024

NT_DIMS = (((1,), (1,)), ((), ()))


def _cparams(sem):
    return pltpu.CompilerParams(dimension_semantics=sem, vmem_limit_bytes=VMEM_LIMIT)


def _dot(a, b):
    return jnp.dot(a, b, preferred_element_type=F32)


def _dot_nt(a, b):
    return lax.dot_general(a, b, NT_DIMS, preferred_element_type=F32)


def _in_proj_kernel(x_ref, w_ref, b_ref, wg_ref, bg_ref,
                    q_ref, k_ref, kb_ref, v_ref, vb_ref, u_ref, vm_ref, om_ref, ga_ref, gb_ref,
                    g_ref):
    j = pl.program_id(1)

    def proj():
        return _dot(x_ref[...].astype(BF16), w_ref[...]) + b_ref[...]

    @pl.when(j == 0)
    def _():
        q_ref[...] = (proj() * (A_HEAD_DIM ** -0.5)).astype(BF16)
        g_ref[...] = _dot(x_ref[...].astype(BF16), wg_ref[...]) + bg_ref[...]

    @pl.when(j == 1)
    def _():
        k_ref[...] = proj()
        kb_ref[...] = k_ref[...].astype(BF16)

    @pl.when(j == 2)
    def _():
        v_ref[...] = proj()
        vb_ref[...] = v_ref[...].astype(BF16)

    for idx, ref in ((3, u_ref), (5, om_ref), (6, ga_ref), (7, gb_ref)):
        @pl.when(j == idx)
        def _(ref=ref):
            ref[...] = proj()

    @pl.when(j == 4)
    def _():
        vm_ref[...] = proj().astype(BF16)


def _in_proj(x, w_main, b_main, w_gate, b_gate):
    T = x.shape[0]
    tm = min(T, 512)
    assert T % tm == 0
    n_i = T // tm
    row = lambda i, j: (i, 0)
    f32o = jax.ShapeDtypeStruct((T, D_MODEL), F32)
    bf16o = jax.ShapeDtypeStruct((T, D_MODEL), BF16)

    def after(group):
        return lambda i, j: (jnp.minimum(i + (j > group).astype(jnp.int32), n_i - 1), 0)

    out_groups = (0, 1, 1, 2, 2, 3, 4, 5, 6, 7)
    return pl.pallas_call(
        _in_proj_kernel,
        grid=(n_i, N_MAIN_GROUPS),
        in_specs=[
            pl.BlockSpec((tm, D_MODEL), row),
            pl.BlockSpec((D_MODEL, D_MODEL), lambda i, j: (0, j)),
            pl.BlockSpec((1, D_MODEL), lambda i, j: (0, j)),
            pl.BlockSpec((D_MODEL, LANES), lambda i, j: (0, 0)),
            pl.BlockSpec((1, LANES), lambda i, j: (0, 0)),
        ],
        out_specs=[pl.BlockSpec((tm, D_MODEL), after(g)) for g in out_groups]
                  + [pl.BlockSpec((tm, LANES), after(0))],
        out_shape=[bf16o, f32o, bf16o, f32o, bf16o, f32o, bf16o, f32o, f32o, f32o,
                   jax.ShapeDtypeStruct((T, LANES), F32)],
        compiler_params=_cparams(("arbitrary", "arbitrary")),
        name="in_proj",
    )(x, w_main, b_main, w_gate, b_gate)


def _lam_value(lq1, lk1, lq2, lk2):
    a = jnp.exp(jnp.sum(lq1 * lk1, axis=1, keepdims=True))
    b = jnp.exp(jnp.sum(lq2 * lk2, axis=1, keepdims=True))
    return a - b + LAM_INIT


def _sub_norm(o, g):
    r = lax.rsqrt(jnp.mean(o * o, axis=1, keepdims=True) + LN_EPS)
    return o * r * g * (1.0 - LAM_INIT)


ATTN_TQ = 256


def _attn_prompt_kernel(lq1_ref, lk1_ref, lq2_ref, lk2_ref, gcol_ref, q_ref, k_ref, vt_ref, o_ref, *, nq):
    i = pl.program_id(2)
    tq = ATTN_TQ
    lam = _lam_value(lq1_ref[...], lk1_ref[...], lq2_ref[...], lk2_ref[...])
    q = q_ref[...]
    lane = lax.broadcasted_iota(jnp.int32, q.shape, 1)
    zero = jnp.zeros_like(q)
    maps = (jnp.where(lane < A_HEAD_DIM, q, zero), jnp.where(lane >= A_HEAD_DIM, q, zero))
    krow = lax.broadcasted_iota(jnp.int32, (tq, tq), 0)
    qcol = lax.broadcasted_iota(jnp.int32, (tq, tq), 1)

    def scores(qm, n):
        lo, hi = n * tq, (n + 1) * tq
        sd = _dot_nt(k_ref[lo:hi, :], qm)
        st = _dot_nt(k_ref[0:lo, :], qm) if n else None
        return sd, st

    def attend(sd, st, n):
        lo, hi = n * tq, (n + 1) * tq
        sd = jnp.where(krow <= qcol, sd, -jnp.inf)
        m = jnp.max(sd, axis=0, keepdims=True)
        if n:
            m = jnp.maximum(m, jnp.max(st, axis=0, keepdims=True))
        pd = jnp.exp(sd - m)
        l = jnp.sum(pd, axis=0, keepdims=True)
        acc = _dot(vt_ref[:, lo:hi], pd.astype(BF16))
        if n:
            pt = jnp.exp(st - m)
            l = l + jnp.sum(pt, axis=0, keepdims=True)
            acc = acc + _dot(vt_ref[:, 0:lo], pt.astype(BF16))
        return acc / l

    for n in range(nq):
        @pl.when(i == n)
        def _(n=n):
            s1, s2 = scores(maps[0], n), scores(maps[1], n)
            o = attend(*s1, n) - lam * attend(*s2, n)
            r = lax.rsqrt(jnp.mean(o * o, axis=0, keepdims=True) + LN_EPS)
            o = o * r * gcol_ref[...] * (1.0 - LAM_INIT)
            o_ref[...] = o.T.astype(BF16)


def _attn_prompt(q, kb, vbt, lam_vecs, subln_gcol, B, S):
    tq = ATTN_TQ
    assert S % tq == 0
    nq = S // tq
    vec = pl.BlockSpec((1, A_HEAD_DIM), lambda b, h, i: (0, 0))
    return pl.pallas_call(
        functools.partial(_attn_prompt_kernel, nq=nq),
        grid=(B, A_HEADS, nq),
        in_specs=[vec, vec, vec, vec,
                  pl.BlockSpec((A_HD2, 1), lambda b, h, i: (0, 0)),
                  pl.BlockSpec((tq, A_HD2), lambda b, h, i: (b * nq + i, h)),
                  pl.BlockSpec((S, A_HD2), lambda b, h, i: (b, h)),
                  pl.BlockSpec((A_HD2, S), lambda b, h, i: (h, b))],
        out_specs=pl.BlockSpec((tq, A_HD2), lambda b, h, i: (b * nq + i, h)),
        out_shape=jax.ShapeDtypeStruct((B * S, D_MODEL), BF16),
        compiler_params=_cparams(("arbitrary", "arbitrary", "arbitrary")),
        name="attn_prompt",
    )(*lam_vecs, subln_gcol, q, kb, vbt)


SAMPLE_ROWS = 8
PAGES_PER_STEP = 8


def _attn_sample_kernel(pt_ref, lq1_ref, lk1_ref, lq2_ref, lk2_ref, g_ref, q_ref, kn_ref, vn_ref, *rest,
                        n_steps, dec_seq):
    pp = PAGES_PER_STEP
    k_refs, v_refs = rest[:pp], rest[pp:2 * pp]
    o_ref, qf_ref, bias_ref, m_ref, l_ref, acc_ref = rest[2 * pp:]
    step_id = pl.program_id(1)
    rows = A_HEADS * SAMPLE_ROWS
    flat = PAGE_SIZE * A_HEADS

    @pl.when(step_id == 0)
    def _():
        q8 = q_ref[0]
        qf = jnp.concatenate([q8[:, h * A_HD2:(h + 1) * A_HD2] for h in range(A_HEADS)], axis=0)
        row = lax.broadcasted_iota(jnp.int32, qf.shape, 0)
        lane = lax.broadcasted_iota(jnp.int32, qf.shape, 1)
        first_map = (row % SAMPLE_ROWS) < dec_seq
        qf_ref[...] = jnp.where(first_map == (lane < A_HEAD_DIM), qf, jnp.zeros_like(qf))
        brow = lax.broadcasted_iota(jnp.int32, (rows, flat), 0)
        bcol = lax.broadcasted_iota(jnp.int32, (rows, flat), 1)
        bias_ref[...] = jnp.where((bcol % A_HEADS) == (brow // SAMPLE_ROWS), 0.0, -jnp.inf)
        m_ref[...] = jnp.full(m_ref.shape, -jnp.inf, F32)
        l_ref[...] = jnp.zeros(l_ref.shape, F32)
        acc_ref[...] = jnp.zeros(acc_ref.shape, F32)

    def update(kfs, vfs, bias):
        ss = [_dot_nt(qf_ref[...], kf) + bias for kf in kfs]
        m = m_ref[...]
        m_new = m
        for s in ss:
            m_new = jnp.maximum(m_new, jnp.max(s, axis=1, keepdims=True))
        alpha = jnp.exp(m - m_new)
        l = alpha * l_ref[...]
        acc = alpha * acc_ref[...]
        for s, vf in zip(ss, vfs):
            p = jnp.exp(s - m_new)
            l = l + jnp.sum(p, axis=1, keepdims=True)
            acc = acc + _dot(p.astype(BF16), vf)
        l_ref[...] = l
        acc_ref[...] = acc
        m_ref[...] = m_new

    update([k_refs[t][0, 0].reshape(flat, A_HD2).astype(BF16) for t in range(pp)],
           [v_refs[t][0, 0].reshape(flat, A_HD2).astype(BF16) for t in range(pp)], bias_ref[...])

    @pl.when(step_id == n_steps - 1)
    def _():
        nnew = SAMPLE_ROWS * A_HEADS
        pad = jnp.zeros((PAGE_SIZE - nnew, A_HD2), BF16)
        kf = jnp.concatenate([kn_ref[0].reshape(nnew, A_HD2).astype(BF16), pad], axis=0)
        vf = jnp.concatenate([vn_ref[0].reshape(nnew, A_HD2).astype(BF16), pad], axis=0)
        row = lax.broadcasted_iota(jnp.int32, (rows, PAGE_SIZE), 0)
        col = lax.broadcasted_iota(jnp.int32, (rows, PAGE_SIZE), 1)
        ok = ((col % A_HEADS) == (row // SAMPLE_ROWS)) & ((col // A_HEADS) <= (row % dec_seq))
        update([kf], [vf], jnp.where(ok, 0.0, -jnp.inf))
        lam = _lam_value(lq1_ref[...], lk1_ref[...], lq2_ref[...], lk2_ref[...])
        g = g_ref[...]
        for h in range(A_HEADS):
            rs = slice(h * SAMPLE_ROWS, (h + 1) * SAMPLE_ROWS)
            o = acc_ref[rs, :] / l_ref[rs, :]
            o = o - lam * pltpu.roll(o, SAMPLE_ROWS - dec_seq, 0)
            o_ref[0, :, h * A_HD2:(h + 1) * A_HD2] = _sub_norm(o, g)


def _attn_sample(q8, kn8, vn8, cache_k, cache_v, page_table, lam_vecs, subln_g):
    DB, n_pages = page_table.shape
    pp = PAGES_PER_STEP
    assert n_pages % pp == 0
    n_steps = n_pages // pp
    dec_seq = SAMPLE_ROWS // 2
    rows = A_HEADS * SAMPLE_ROWS
    vec = pl.BlockSpec((1, A_HEAD_DIM), lambda b, s, pt: (0, 0))
    per_b = pl.BlockSpec((1, SAMPLE_ROWS, D_MODEL), lambda b, s, pt: (b, 0, 0))
    new_kv = pl.BlockSpec((1, SAMPLE_ROWS, A_HEADS, A_HD2), lambda b, s, pt: (b, 0, 0, 0))

    def page_spec(t):
        return pl.BlockSpec((1, 1, PAGE_SIZE, A_HEADS, A_HD2),
                            lambda b, s, pt: (0, pt[b, s * pp + t], 0, 0, 0))

    grid_spec = pltpu.PrefetchScalarGridSpec(
        num_scalar_prefetch=1,
        grid=(DB, n_steps),
        in_specs=[vec, vec, vec, vec, pl.BlockSpec((1, A_HD2), lambda b, s, pt: (0, 0)),
                  per_b, new_kv, new_kv]
                 + [page_spec(t) for t in range(pp)] + [page_spec(t) for t in range(pp)],
        out_specs=per_b,
        scratch_shapes=[pltpu.VMEM((rows, A_HD2), BF16),
                        pltpu.VMEM((rows, PAGE_SIZE * A_HEADS), F32),
                        pltpu.VMEM((rows, 1), F32), pltpu.VMEM((rows, 1), F32),
                        pltpu.VMEM((rows, A_HD2), F32)],
    )
    return pl.pallas_call(
        functools.partial(_attn_sample_kernel, n_steps=n_steps, dec_seq=dec_seq),
        grid_spec=grid_spec,
        out_shape=jax.ShapeDtypeStruct((DB, SAMPLE_ROWS, D_MODEL), F32),
        compiler_params=_cparams(("arbitrary", "arbitrary")),
        name="attn_sample",
    )(page_table, *lam_vecs, subln_g, q8, kn8, vn8, *([cache_k] * pp), *([cache_v] * pp))


CONV_HALO = 8
MLSTM_HEADS_PER_STEP = 2


def _log_sigmoid(x):
    return -(jnp.maximum(-x, 0.0) + jnp.log1p(jnp.exp(-jnp.abs(x))))


def _mlstm_kernel(u_ref, halo_ref, vm_ref, om_ref, icol_ref, fcol_ref, irow_ref, frow_ref,
                  wconv_ref, bconv_ref, wq_ref, wk_ref, wkt_ref, g_ref, c0_ref, n0_ref, m0_ref,
                  hb_ref, c_ref, n_ref, m_ref,
                  ext_ref, uc_ref, q_ref, k_ref, ct_ref, nrow_ref, *, S, L):
    kscale = M_HEAD_DIM ** -0.5
    dh = M_HEAD_DIM
    heads = range(MLSTM_HEADS_PER_STEP)
    hcols = [slice(hh * dh, (hh + 1) * dh) for hh in heads]
    ext_ref[0:CONV_HALO, :] = halo_ref[0]
    ext_ref[CONV_HALO:CONV_HALO + S, :] = u_ref[...]
    rb = min(S, 512)
    for r0 in range(0, S, rb):
        acc = jnp.broadcast_to(bconv_ref[...], (rb, ext_ref.shape[1]))
        for jj in range(M_CONV):
            off = CONV_HALO - (M_CONV - 1) + jj + r0
            acc = acc + ext_ref[off:off + rb, :] * wconv_ref[jj:jj + 1, :]
        uc = (acc * jax.nn.sigmoid(acc)).astype(BF16)
        uc_ref[r0:r0 + rb, :] = uc
        for hh in heads:
            q_ref[r0:r0 + rb, hcols[hh]] = _dot(uc[:, hcols[hh]], wq_ref[hh]).astype(BF16)
            k_ref[r0:r0 + rb, hcols[hh]] = (_dot(uc[:, hcols[hh]], wk_ref[hh]) * kscale).astype(BF16)

    for hh in heads:
        ct_ref[hh] = c0_ref[0, hh].T
        nrow_ref[hh] = n0_ref[0, hh]
    ti = lax.broadcasted_iota(jnp.int32, (L, L), 0)
    si = lax.broadcasted_iota(jnp.int32, (L, L), 1)
    tril = si <= ti
    gain = g_ref[...]

    def chunk(c, m_prevs):
        r0 = pl.multiple_of(c * L, L)
        rows = pl.ds(r0, L)
        qcs = [q_ref[rows, hcols[hh]] for hh in heads]
        vcs = [vm_ref[rows, hcols[hh]] for hh in heads]
        kcts = [(_dot_nt(wkt_ref[hh], uc_ref[rows, hcols[hh]]) * kscale).astype(BF16) for hh in heads]
        sqk = [_dot(qcs[hh], kcts[hh]) for hh in heads]
        qct = [_dot(qcs[hh], ct_ref[hh].astype(BF16)) for hh in heads]
        gm = []
        for hh in heads:
            m_prev = m_prevs[hh]
            icol = icol_ref[0, hh, rows, :]
            fcol = _log_sigmoid(fcol_ref[0, hh, rows, :])
            irow = irow_ref[0, hh, pl.ds(c, 1), :]
            frow = _log_sigmoid(frow_ref[0, hh, pl.ds(c, 1), :])
            b_col = jnp.sum(jnp.where(tril, frow, 0.0), axis=1, keepdims=True)
            b_row = jnp.sum(jnp.where(ti <= si, fcol, 0.0), axis=0, keepdims=True)
            dmat = jnp.where(tril, b_col - b_row + irow, -jnp.inf)
            m_t = jnp.maximum(b_col + m_prev, jnp.max(dmat, axis=1, keepdims=True))
            wmat = jnp.exp(dmat - m_t)
            inter = jnp.exp(b_col + m_prev - m_t)
            m_new = m_t[L - 1:L, :]
            b_last = b_col[L - 1:L, :]
            decay = jnp.exp(b_last + m_prev - m_new)
            w_col = jnp.exp(b_last - b_col + icol - m_new)
            gm.append((m_t, wmat, inter, m_new, decay, w_col))
        sws = [sqk[hh] * gm[hh][1] for hh in heads]
        wvs = [(gm[hh][5] * vcs[hh].astype(F32)).astype(BF16) for hh in heads]
        swv = [_dot(sws[hh].astype(BF16), vcs[hh]) for hh in heads]
        upd = [_dot(kcts[hh], wvs[hh]) for hh in heads]
        m_news = []
        for hh in heads:
            m_t, wmat, inter, m_new, decay, w_col = gm[hh]
            qc = qcs[hh]
            kc = k_ref[rows, hcols[hh]]
            num = inter * qct[hh] + swv[hh]
            qn = jnp.sum(qc.astype(F32) * nrow_ref[hh], axis=1, keepdims=True)
            dot = inter * qn + jnp.sum(sws[hh], axis=1, keepdims=True)
            h = num / jnp.maximum(jnp.abs(dot), jnp.exp(-m_t))
            ct_ref[hh] = decay * ct_ref[hh] + upd[hh]
            nrow_ref[hh] = decay * nrow_ref[hh] + jnp.sum(w_col * kc.astype(F32), axis=0, keepdims=True)
            mu = jnp.mean(h, axis=1, keepdims=True)
            hc = h - mu
            var = jnp.mean(hc * hc, axis=1, keepdims=True)
            hn = hc * lax.rsqrt(var + LN_EPS) * gain[:, hcols[hh]]
            hb_ref[rows, hcols[hh]] = (jax.nn.sigmoid(om_ref[rows, hcols[hh]]) * hn).astype(BF16)
            m_news.append(m_new)
        return tuple(m_news)

    m_fin = lax.fori_loop(0, S // L, chunk, tuple(m0_ref[0, hh][:, 0:1] for hh in heads))
    for hh in heads:
        c_ref[0, hh] = ct_ref[hh].T
        n_ref[0, hh] = nrow_ref[hh]
        m_ref[0, hh] = jnp.broadcast_to(m_fin[hh], (1, LANES))


def _mlstm(u, halo, vm, om, icol, fcol, irow, frow, w_conv, b_conv, wq, wk, wkt, mnorm_g,
           c0, n0, m0, B, S):
    L = M_CHUNK
    nc = S // L
    H, dh = M_HEADS, M_HEAD_DIM
    hp = MLSTM_HEADS_PER_STEP
    assert H % hp == 0
    seq = pl.BlockSpec((S, hp * dh), lambda b, h: (b, h))
    col = pl.BlockSpec((1, hp, S, 1), lambda b, h: (b, h, 0, 0))
    rowg = pl.BlockSpec((1, hp, nc, L), lambda b, h: (b, h, 0, 0))
    wmat = pl.BlockSpec((hp, dh, dh), lambda b, h: (h, 0, 0))
    state = pl.BlockSpec((1, hp, dh, dh), lambda b, h: (b, h, 0, 0))
    nvec = pl.BlockSpec((1, hp, 1, dh), lambda b, h: (b, h, 0, 0))
    mvec = pl.BlockSpec((1, hp, 1, LANES), lambda b, h: (b, h, 0, 0))
    return pl.pallas_call(
        functools.partial(_mlstm_kernel, S=S, L=L),
        grid=(B, H // hp),
        in_specs=[seq,
                  pl.BlockSpec((1, CONV_HALO, hp * dh), lambda b, h: (b, 0, h)),
                  seq, seq, col, col, rowg, rowg,
                  pl.BlockSpec((M_CONV, hp * dh), lambda b, h: (0, h)),
                  pl.BlockSpec((1, hp * dh), lambda b, h: (0, h)),
                  wmat, wmat, wmat,
                  pl.BlockSpec((1, hp * dh), lambda b, h: (0, h)),
                  state, nvec, mvec],
        out_specs=[seq, state, nvec, mvec],
        out_shape=[jax.ShapeDtypeStruct((B * S, D_MODEL), BF16),
                   jax.ShapeDtypeStruct((B, H, dh, dh), F32),
                   jax.ShapeDtypeStruct((B, H, 1, dh), F32),
                   jax.ShapeDtypeStruct((B, H, 1, LANES), F32)],
        scratch_shapes=[pltpu.VMEM((S + CONV_HALO, hp * dh), F32),
                        pltpu.VMEM((S, hp * dh), BF16), pltpu.VMEM((S, hp * dh), BF16),
                        pltpu.VMEM((S, hp * dh), BF16),
                        pltpu.VMEM((hp, dh, dh), F32), pltpu.VMEM((hp, 1, dh), F32)],
        compiler_params=_cparams(("arbitrary", "arbitrary")),
        name="mlstm",
    )(u, halo, vm, om, icol, fcol, irow, frow, w_conv, b_conv, wq, wk, wkt, mnorm_g, c0, n0, m0)


def _layer_norm_rows(z, g, b):
    mu = jnp.mean(z, axis=1, keepdims=True)
    zc = z - mu
    var = jnp.mean(zc * zc, axis=1, keepdims=True)
    return zc * lax.rsqrt(var + LN_EPS) * g + b


def _merge_kernel(x_ref, oa_ref, hb_ref, ga_ref, gb_ref, wa_ref, wb_ref, wo_ref, g_ref, b_ref,
                  h1t_ref, h1tb_ref):
    ya = _dot(oa_ref[...], wa_ref[...])
    yb = _dot(hb_ref[...], wb_ref[...])
    merged = jax.nn.sigmoid(ga_ref[...]) * ya + jax.nn.sigmoid(gb_ref[...]) * yb
    out = _dot(merged.astype(BF16), wo_ref[...])
    h1 = _layer_norm_rows(ALPHA * x_ref[...] + out, g_ref[...], b_ref[...])
    h1t = h1.T
    h1t_ref[...] = h1t
    h1tb_ref[...] = h1t.astype(BF16)


def _merge(x, oa, hb, ga, gb, wa, wb, wo, ln_g, ln_b):
    T = x.shape[0]
    tm = min(T, 512)
    tok = pl.BlockSpec((tm, D_MODEL), lambda i: (i, 0))
    wsp = pl.BlockSpec((D_MODEL, D_MODEL), lambda i: (0, 0))
    vsp = pl.BlockSpec((1, D_MODEL), lambda i: (0, 0))
    tsp = pl.BlockSpec((D_MODEL, tm), lambda i: (0, i))
    return pl.pallas_call(
        _merge_kernel,
        grid=(T // tm,),
        in_specs=[tok, tok, tok, tok, tok, wsp, wsp, wsp, vsp, vsp],
        out_specs=[tsp, tsp],
        out_shape=[jax.ShapeDtypeStruct((D_MODEL, T), F32), jax.ShapeDtypeStruct((D_MODEL, T), BF16)],
        compiler_params=_cparams(("arbitrary",)),
        name="merge_ln1",
    )(x, oa, hb, ga, gb, wa, wb, wo, ln_g, ln_b)


NOT_SELECTED = 255.0


def _top16_ranks(s, iota):
    rank = jnp.full(s.shape, NOT_SELECTED, F32)
    tops = []
    for r in range(P_TOPK):
        mx = jnp.max(s, axis=0, keepdims=True)
        idx = jnp.min(jnp.where(s == mx, iota, float(P_NKEYS)), axis=0, keepdims=True)
        sel = iota == idx
        rank = jnp.where(sel, float(r), rank)
        s = jnp.where(sel, -jnp.inf, s)
        tops.append(mx)
    return rank, jnp.concatenate(tops, axis=0)


def _top16(s, iota):
    rank = jnp.full(s.shape, NOT_SELECTED, F32)
    tops = []
    for r in range(P_TOPK):
        mx = jnp.max(s, axis=0, keepdims=True)
        eq = s == mx
        rank = jnp.where(eq, float(r), rank)
        s = jnp.where(eq, -jnp.inf, s)
        tops.append(mx)
    n_ranked = jnp.sum(jnp.where(rank < float(P_TOPK), 1.0, 0.0), axis=0, keepdims=True)
    return rank, jnp.concatenate(tops, axis=0), n_ranked


def _pair_merge(sv1, sv2, riota):
    cnt = jnp.zeros(sv1.shape, F32)
    front = sv1 + sv2[0:1, :]
    top0 = front[0:1, :]
    z = jnp.zeros_like(top0)
    for _ in range(P_TOPK):
        fm = jnp.max(front, axis=0, keepdims=True)
        idx = jnp.min(jnp.where(front == fm, riota, float(P_TOPK)), axis=0, keepdims=True)
        sel = riota == idx
        z = z + jnp.exp(fm - top0)
        cnt = cnt + jnp.where(sel, 1.0, 0.0)
        csel = jnp.sum(jnp.where(sel, cnt, 0.0), axis=0, keepdims=True)
        v1 = jnp.sum(jnp.where(sel, sv1, 0.0), axis=0, keepdims=True)
        v2 = jnp.sum(jnp.where(riota == csel, sv2, 0.0), axis=0, keepdims=True)
        nxt = jnp.where(csel < float(P_TOPK), v1 + v2, -jnp.inf)
        front = jnp.where(sel, nxt, front)
    return cnt, z


def _route_kernel(h1t_ref, wpqt_ref, keys_ref, e1_ref, lf_ref, e2_ref, r2_ref, qt_ref, *, tt):
    qt_ref[...] = _dot(wpqt_ref[...], h1t_ref[...]).astype(BF16)
    kiota = lax.broadcasted_iota(jnp.int32, (P_NKEYS, LANES), 0).astype(F32)
    riota = lax.broadcasted_iota(jnp.int32, (P_TOPK, LANES), 0).astype(F32)

    def head(h, carry):
        r1 = pl.multiple_of(h * 2 * P_HALF_DIM, P_HALF_DIM)
        r2 = pl.multiple_of(h * 2 * P_HALF_DIM + P_HALF_DIM, P_HALF_DIM)
        slabs = [slice(w0, w0 + LANES) for w0 in range(0, tt, LANES)]
        scores = []
        for ws in slabs:
            scores.append(_dot(keys_ref[2 * h], qt_ref[pl.ds(r1, P_HALF_DIM), ws]))
            scores.append(_dot(keys_ref[2 * h + 1], qt_ref[pl.ds(r2, P_HALF_DIM), ws]))
        quick = [_top16(s, kiota) for s in scores]
        n_ranked = quick[0][2]
        for q in quick[1:]:
            n_ranked = jnp.maximum(n_ranked, q[2])
        tied = jnp.max(n_ranked) > float(P_TOPK)
        ranked = lax.cond(tied,
                          lambda: tuple(_top16_ranks(s, kiota) for s in scores),
                          lambda: tuple((q[0], q[1]) for q in quick))
        for si, ws in enumerate(slabs):
            s1, s2 = scores[2 * si], scores[2 * si + 1]
            (rank1, sv1), (rank2, sv2) = ranked[2 * si], ranked[2 * si + 1]
            cnt, z = _pair_merge(sv1, sv2, riota)
            in1 = rank1 < float(P_TOPK)
            in2 = rank2 < float(P_TOPK)
            e1 = jnp.where(in1, jnp.exp(s1 - sv1[0:1, :]) / z, 0.0)
            e2 = jnp.where(in2, jnp.exp(s2 - sv2[0:1, :]), 0.0)
            lf = jnp.zeros_like(s1)
            for r in range(P_TOPK):
                lf = jnp.where(rank1 == float(r), cnt[r:r + 1, :], lf)
            e1_ref[h, :, ws] = e1
            lf_ref[h, :, ws] = lf
            e2_ref[h, :, ws] = e2.astype(BF16)
            r2_ref[h, :, ws] = rank2.astype(BF16)
        return carry

    lax.fori_loop(0, P_HEADS, head, 0)


def _route(h1tb, wpqt, keys):
    T = h1tb.shape[1]
    tt = min(T, 512)
    osp = pl.BlockSpec((P_HEADS, P_NKEYS, tt), lambda i: (0, 0, i))
    of32 = jax.ShapeDtypeStruct((P_HEADS, P_NKEYS, T), F32)
    obf = jax.ShapeDtypeStruct((P_HEADS, P_NKEYS, T), BF16)
    return pl.pallas_call(
        functools.partial(_route_kernel, tt=tt),
        grid=(T // tt,),
        in_specs=[pl.BlockSpec((D_MODEL, tt), lambda i: (0, i)),
                  pl.BlockSpec((2 * P_HEADS * P_HALF_DIM, D_MODEL), lambda i: (0, 0)),
                  pl.BlockSpec((2 * P_HEADS, P_NKEYS, P_HALF_DIM), lambda i: (0, 0, 0))],
        out_specs=[osp, osp, osp, osp],
        out_shape=[of32, of32, obf, obf],
        scratch_shapes=[pltpu.VMEM((2 * P_HEADS * P_HALF_DIM, tt), BF16)],
        compiler_params=_cparams(("arbitrary",)),
        name="peer_route",
    )(h1tb, wpqt, keys)


PEER_EXPERT_BLOCK = 1024
PEER_SUB_BLOCK = 256


def _peer_kernel(h1t_ref, h1tb_ref, e1_ref, lf_ref, e2_ref, r2_ref, pu_ref, pvt_ref, g_ref, b_ref,
                 y_ref, acc_ref, w_ref, *, n_eb):
    eb = pl.program_id(1)
    slot = eb % 2
    rows_per_block = PEER_EXPERT_BLOCK // P_NKEYS
    sub = PEER_SUB_BLOCK
    n_sub = PEER_EXPERT_BLOCK // sub
    tt = acc_ref.shape[1]
    n_parts = 2 if tt % (2 * LANES) == 0 else 1
    pw = tt // n_parts

    def scores(sb):
        return _dot(pu_ref[sb * sub:(sb + 1) * sub, :], h1tb_ref[...])

    def out_part(p):
        return _dot(pvt_ref[...], w_ref[1 - slot, :, p * pw:(p + 1) * pw])

    def gated(a, sb):
        act = (0.5 * a * (1.0 + lax.erf(a * (2.0 ** -0.5)))).astype(BF16)
        gates = []
        for ci in range(sub // P_NKEYS):
            c = eb * rows_per_block + sb * (sub // P_NKEYS) + ci
            gate = jnp.zeros((P_NKEYS, tt), BF16)
            for h in range(P_HEADS):
                lrow = lf_ref[h, pl.ds(c, 1), :].astype(BF16)
                erow = e1_ref[h, pl.ds(c, 1), :].astype(BF16)
                picked = jnp.where(r2_ref[h] < lrow, e2_ref[h], jnp.zeros_like(gate))
                gate = gate + picked * erow
            gates.append(gate)
        w_ref[slot, sb * sub:(sb + 1) * sub, :] = jnp.concatenate(gates, axis=0) * act

    def build(with_out):
        a = [scores(0), scores(1)] + [None] * (n_sub - 2)
        outs = []
        for sb in range(n_sub):
            if with_out and sb < n_parts:
                outs.append(out_part(sb))
            gated(a[sb], sb)
            if sb + 2 < n_sub:
                a[sb + 2] = scores(sb + 2)
        for p, o in enumerate(outs):
            acc_ref[:, p * pw:(p + 1) * pw] += o

    @pl.when(eb == 0)
    def _():
        acc_ref[...] = jnp.zeros(acc_ref.shape, F32)
        build(False)

    @pl.when((eb > 0) & (eb < n_eb))
    def _():
        build(True)

    @pl.when(eb == n_eb)
    def _():
        out = jnp.concatenate([out_part(p) for p in range(n_parts)], axis=1)
        z = (ALPHA * h1t_ref[...] + acc_ref[...] + out).T
        y_ref[...] = _layer_norm_rows(z, g_ref[...], b_ref[...])


def _peer(h1t, h1tb, e1, lf, e2, r2, pu, pvt, ln_g, ln_b):
    T = h1t.shape[1]
    tt = min(T, 512)
    n_eb = P_EXPERTS // PEER_EXPERT_BLOCK
    assert PEER_EXPERT_BLOCK // PEER_SUB_BLOCK >= 2
    tsp = pl.BlockSpec((D_MODEL, tt), lambda i, e: (0, i))
    rsp = pl.BlockSpec((P_HEADS, P_NKEYS, tt), lambda i, e: (0, 0, i))
    vsp = pl.BlockSpec((1, D_MODEL), lambda i, e: (0, 0))
    return pl.pallas_call(
        functools.partial(_peer_kernel, n_eb=n_eb),
        grid=(T // tt, n_eb + 1),
        in_specs=[tsp, tsp, rsp, rsp, rsp, rsp,
                  pl.BlockSpec((PEER_EXPERT_BLOCK, D_MODEL), lambda i, e: (jnp.minimum(e, n_eb - 1), 0)),
                  pl.BlockSpec((D_MODEL, PEER_EXPERT_BLOCK), lambda i, e: (0, jnp.maximum(e - 1, 0))),
                  vsp, vsp],
        out_specs=pl.BlockSpec((tt, D_MODEL), lambda i, e: (i, 0)),
        out_shape=jax.ShapeDtypeStruct((T, D_MODEL), F32),
        scratch_shapes=[pltpu.VMEM((D_MODEL, tt), F32), pltpu.VMEM((2, PEER_EXPERT_BLOCK, tt), BF16)],
        compiler_params=_cparams(("arbitrary", "arbitrary")),
        name="peer_dense",
    )(h1t, h1tb, e1, lf, e2, r2, pu, pvt, ln_g, ln_b)


def _gate_layouts(g, B, S_real, S_pad):
    H, L = M_HEADS, M_CHUNK
    gi = g[:, :H].reshape(B, S_real, H)
    gf = g[:, H:2 * H].reshape(B, S_real, H)
    if S_pad != S_real:
        pw = ((0, 0), (0, S_pad - S_real), (0, 0))
        gi = jnp.pad(gi, pw, constant_values=PAD_LOG_INPUT_GATE)
        gf = jnp.pad(gf, pw, constant_values=-PAD_LOG_INPUT_GATE)
    gi = jnp.swapaxes(gi, 1, 2)
    gf = jnp.swapaxes(gf, 1, 2)
    col = lambda a: a.reshape(B, H, S_pad, 1)
    row = lambda a: a.reshape(B, H, S_pad // L, L)
    return col(gi), col(gf), row(gi), row(gf)


def _pad_seq(a, B, S_real, S_pad):
    if S_pad == S_real:
        return a
    a = a.reshape(B, S_real, a.shape[-1])
    return jnp.pad(a, ((0, 0), (0, S_pad - S_real), (0, 0))).reshape(B * S_pad, a.shape[-1])


def _group(x3, attend, state, W):
    B, S, _ = x3.shape
    T = B * S
    x = x3.reshape(T, D_MODEL)
    q, k, kb, v, vb, u, vm, om, ga, gb, g = _in_proj(x, W["w_main"], W["b_main"], W["w_gate"], W["b_gate"])
    oa = attend(q, k, v, kb, vb)

    c0, n0, m0, conv0 = state
    S_pad = -(-S // M_CHUNK) * M_CHUNK
    icol, fcol, irow, frow = _gate_layouts(g, B, S, S_pad)
    halo = jnp.pad(conv0, ((0, 0), (CONV_HALO - (M_CONV - 1), 0), (0, 0)))
    hb, c_new, n_new, m_new = _mlstm(
        _pad_seq(u, B, S, S_pad), halo, _pad_seq(vm, B, S, S_pad), _pad_seq(om, B, S, S_pad),
        icol, fcol, irow, frow, W["w_conv"], W["b_conv"], W["w_qm"], W["w_km"], W["w_kmt"], W["mnorm_g"],
        c0, n0.reshape(B, M_HEADS, 1, M_HEAD_DIM),
        jnp.broadcast_to(m0[:, :, None, None], (B, M_HEADS, 1, LANES)), B, S_pad)
    if S_pad != S:
        hb = hb.reshape(B, S_pad, D_MODEL)[:, :S].reshape(T, D_MODEL)
    u3 = u.reshape(B, S, D_MODEL)
    conv_new = jnp.concatenate([conv0, u3], axis=1)[:, S:]

    h1t, h1tb = _merge(x, oa, hb, ga, gb, W["w_a"], W["w_b"], W["w_o"], W["ln1_g"], W["ln1_b"])
    e1, lf, e2, r2 = _route(h1tb, W["w_pqt"], W["p_keys"])
    y = _peer(h1t, h1tb, e1, lf, e2, r2, W["p_u"], W["p_vt"], W["ln2_g"], W["ln2_b"])

    new_state = (k.reshape(1, B, S, A_HEADS, A_HD2), v.reshape(1, B, S, A_HEADS, A_HD2),
                 c_new[None], n_new.reshape(1, B, M_HEADS, M_HEAD_DIM), m_new[None, :, :, 0, 0],
                 conv_new[None])
    return y.reshape(B, S, D_MODEL), new_state


def kernel(x_prompt, x_sample, cache_k, cache_v, state_C, state_n, state_m, state_conv, page_table,
           w_in, b_in, lam_q1, lam_k1, lam_q2, lam_k2, subln_g, w_conv, b_conv, w_qm, w_km, mnorm_g,
           w_a, w_b, w_o, ln1_g, ln1_b, w_pq, p_keys, p_u, p_v, ln2_g, ln2_b):
    assert w_in.shape[0] == DEPTH
    B, S, _ = x_prompt.shape
    DB, DS, _ = x_sample.shape
    assert 2 * DS == SAMPLE_ROWS

    n_pre = 6 * D_MODEL
    wi, bi = w_in[0], b_in[0]
    W = {
        "w_main": jnp.concatenate([wi[:, :n_pre], wi[:, n_pre + GATE_COLS:]], axis=1).astype(BF16),
        "b_main": jnp.concatenate([bi[:n_pre], bi[n_pre + GATE_COLS:]])[None],
        "w_gate": jnp.pad(wi[:, n_pre:n_pre + GATE_COLS], ((0, 0), (0, LANES - GATE_COLS))).astype(BF16),
        "b_gate": jnp.pad(bi[n_pre:n_pre + GATE_COLS], (0, LANES - GATE_COLS))[None],
        "w_conv": w_conv[0], "b_conv": b_conv[0][None],
        "w_qm": w_qm[0].astype(BF16), "w_km": w_km[0].astype(BF16),
        "w_kmt": jnp.swapaxes(w_km[0], 1, 2).astype(BF16),
        "mnorm_g": mnorm_g[0][None],
        "w_a": w_a[0].astype(BF16), "w_b": w_b[0].astype(BF16), "w_o": w_o[0].astype(BF16),
        "ln1_g": ln1_g[0][None], "ln1_b": ln1_b[0][None],
        "w_pqt": w_pq[0].T.astype(BF16),
        "p_keys": p_keys[0].reshape(2 * P_HEADS, P_NKEYS, P_HALF_DIM).astype(BF16),
        "p_u": p_u[0].astype(BF16), "p_vt": p_v[0].T.astype(BF16),
        "ln2_g": ln2_g[0][None], "ln2_b": ln2_b[0][None],
    }
    lam_vecs = (lam_q1[0][None], lam_k1[0][None], lam_q2[0][None], lam_k2[0][None])
    sg = subln_g[0][None]

    def attend_prompt(q, k, v, kb, vb):
        return _attn_prompt(q, kb, vb.T, lam_vecs, subln_g[0][:, None], B, S)

    def attend_sample(q, k, v, kb, vb):
        q3 = q.reshape(DB, DS, D_MODEL)
        q8 = jnp.concatenate([q3, q3], axis=1)
        padn = ((0, 0), (0, SAMPLE_ROWS - DS), (0, 0), (0, 0))
        kn8 = jnp.pad(k.reshape(DB, DS, A_HEADS, A_HD2), padn)
        vn8 = jnp.pad(v.reshape(DB, DS, A_HEADS, A_HD2), padn)
        o8 = _attn_sample(q8, kn8, vn8, cache_k, cache_v, page_table, lam_vecs, sg)
        return o8[:, :DS].reshape(DB * DS, D_MODEL).astype(BF16)

    zero_state = (jnp.zeros((B, M_HEADS, M_HEAD_DIM, M_HEAD_DIM), F32),
                  jnp.zeros((B, M_HEADS, M_HEAD_DIM), F32),
                  jnp.zeros((B, M_HEADS), F32),
                  jnp.zeros((B, M_CONV - 1, D_MODEL), F32))
    yp, sp = _group(x_prompt, attend_prompt, zero_state, W)
    ys, ss = _group(x_sample, attend_sample,
                    (state_C[0], state_n[0], state_m[0], state_conv[0]), W)
    return (yp, ys) + sp + ss
```

```python
import functools
import math

import jax
import jax.numpy as jnp
from jax import lax
from jax.experimental import pallas as pl
from jax.experimental.pallas import tpu as pltpu

F32 = jnp.float32
BF16 = jnp.bfloat16

D_MODEL = 1024
A_HEADS = 8
A_HEAD_DIM = 64
A_HD2 = 2 * A_HEAD_DIM
M_HEADS = 4
M_HEAD_DIM = 256
M_CONV = 4
M_CHUNK = 64
P_HEADS = 8
P_NKEYS = 128
P_HALF_DIM = 128
P_TOPK = 16
P_EXPERTS = P_NKEYS * P_NKEYS
PAGE_SIZE = 128
DEPTH = 1
ALPHA = (2.0 * DEPTH) ** 0.25
LN_EPS = 1e-5
PAD_LOG_INPUT_GATE = -1e30
LAM_INIT = 0.8 - 0.6 * math.exp(-0.3 * 0)

N_MAIN_GROUPS = 8
GATE_COLS = 2 * M_HEADS
LANES = 128
VMEM_LIMIT = 56 * 1024 * 1024

NT_DIMS = (((1,), (1,)), ((), ()))


def _cparams(sem):
    return pltpu.CompilerParams(dimension_semantics=sem, vmem_limit_bytes=VMEM_LIMIT)


def _dot(a, b):
    return jnp.dot(a, b, preferred_element_type=F32)


def _dot_nt(a, b):
    return lax.dot_general(a, b, NT_DIMS, preferred_element_type=F32)


def _in_proj_kernel(x_ref, w_ref, b_ref, wg_ref, bg_ref,
                    q_ref, k_ref, kb_ref, v_ref, u_ref, vm_ref, om_ref, ga_ref, gb_ref,
                    g_ref, vbt_ref):
    xb = x_ref[...].astype(BF16)

    def proj(j):
        cols = slice(j * D_MODEL, (j + 1) * D_MODEL)
        return _dot(xb, w_ref[:, cols]) + b_ref[:, cols]

    q_ref[...] = (proj(0) * (A_HEAD_DIM ** -0.5)).astype(BF16)
    g_ref[...] = _dot(xb, wg_ref[...]) + bg_ref[...]
    k_ref[...] = proj(1)
    kb_ref[...] = k_ref[...].astype(BF16)
    v_ref[...] = proj(2)
    vbt_ref[...] = v_ref[...].T.astype(BF16)
    u_ref[...] = proj(3)
    vm_ref[...] = proj(4).astype(BF16)
    om_ref[...] = proj(5)
    ga_ref[...] = proj(6)
    gb_ref[...] = proj(7)


IN_PROJ_ROWS = 256


def _in_proj(x, w_main, b_main, w_gate, b_gate):
    T = x.shape[0]
    tm = min(T, IN_PROJ_ROWS)
    assert T % tm == 0
    row = lambda i: (i, 0)
    whole = lambda i: (0, 0)
    f32o = jax.ShapeDtypeStruct((T, D_MODEL), F32)
    bf16o = jax.ShapeDtypeStruct((T, D_MODEL), BF16)
    ncol = N_MAIN_GROUPS * D_MODEL
    return pl.pallas_call(
        _in_proj_kernel,
        grid=(T // tm,),
        in_specs=[
            pl.BlockSpec((tm, D_MODEL), row),
            pl.BlockSpec((D_MODEL, ncol), whole),
            pl.BlockSpec((1, ncol), whole),
            pl.BlockSpec((D_MODEL, LANES), whole),
            pl.BlockSpec((1, LANES), whole),
        ],
        out_specs=[pl.BlockSpec((tm, D_MODEL), row)] * 9 + [pl.BlockSpec((tm, LANES), row),
                                                           pl.BlockSpec((D_MODEL, tm), lambda i: (0, i))],
        out_shape=[bf16o, f32o, bf16o, f32o, f32o, bf16o, f32o, f32o, f32o,
                   jax.ShapeDtypeStruct((T, LANES), F32), jax.ShapeDtypeStruct((D_MODEL, T), BF16)],
        compiler_params=_cparams(("arbitrary",)),
        name="in_proj",
    )(x, w_main, b_main, w_gate, b_gate)


def _lam_value(lq1, lk1, lq2, lk2):
    a = jnp.exp(jnp.sum(lq1 * lk1, axis=1, keepdims=True))
    b = jnp.exp(jnp.sum(lq2 * lk2, axis=1, keepdims=True))
    return a - b + LAM_INIT


def _sub_norm(o, g):
    r = lax.rsqrt(jnp.mean(o * o, axis=1, keepdims=True) + LN_EPS)
    return o * r * g * (1.0 - LAM_INIT)


ATTN_TQ = 256


def _attn_prompt_kernel(lq1_ref, lk1_ref, lq2_ref, lk2_ref, gcol_ref, q_ref, k_ref, vt_ref, o_ref, *, nq):
    i = pl.program_id(2)
    tq = ATTN_TQ
    lam = _lam_value(lq1_ref[...], lk1_ref[...], lq2_ref[...], lk2_ref[...])
    q = q_ref[...]
    lane = lax.broadcasted_iota(jnp.int32, q.shape, 1)
    zero = jnp.zeros_like(q)
    maps = (jnp.where(lane < A_HEAD_DIM, q, zero), jnp.where(lane >= A_HEAD_DIM, q, zero))
    krow = lax.broadcasted_iota(jnp.int32, (tq, tq), 0)
    qcol = lax.broadcasted_iota(jnp.int32, (tq, tq), 1)

    def scores(qm, n):
        lo, hi = n * tq, (n + 1) * tq
        sd = _dot_nt(k_ref[lo:hi, :], qm)
        st = _dot_nt(k_ref[0:lo, :], qm) if n else None
        return sd, st

    def attend(sd, st, n):
        lo, hi = n * tq, (n + 1) * tq
        sd = jnp.where(krow <= qcol, sd, -jnp.inf)
        m = jnp.max(sd, axis=0, keepdims=True)
        if n:
            m = jnp.maximum(m, jnp.max(st, axis=0, keepdims=True))
        pd = jnp.exp(sd - m)
        l = jnp.sum(pd, axis=0, keepdims=True)
        acc = _dot(vt_ref[:, lo:hi], pd.astype(BF16))
        if n:
            pt = jnp.exp(st - m)
            l = l + jnp.sum(pt, axis=0, keepdims=True)
            acc = acc + _dot(vt_ref[:, 0:lo], pt.astype(BF16))
        return acc / l

    for n in range(nq):
        @pl.when(i == n)
        def _(n=n):
            s1, s2 = scores(maps[0], n), scores(maps[1], n)
            o = attend(*s1, n) - lam * attend(*s2, n)
            r = lax.rsqrt(jnp.mean(o * o, axis=0, keepdims=True) + LN_EPS)
            o = o * r * gcol_ref[...] * (1.0 - LAM_INIT)
            o_ref[...] = o.T.astype(BF16)


def _attn_prompt(q, kb, vbt, lam_vecs, subln_gcol, B, S):
    tq = ATTN_TQ
    assert S % tq == 0
    nq = S // tq
    vec = pl.BlockSpec((1, A_HEAD_DIM), lambda b, h, i: (0, 0))
    return pl.pallas_call(
        functools.partial(_attn_prompt_kernel, nq=nq),
        grid=(B, A_HEADS, nq),
        in_specs=[vec, vec, vec, vec,
                  pl.BlockSpec((A_HD2, 1), lambda b, h, i: (0, 0)),
                  pl.BlockSpec((tq, A_HD2), lambda b, h, i: (b * nq + i, h)),
                  pl.BlockSpec((S, A_HD2), lambda b, h, i: (b, h)),
                  pl.BlockSpec((A_HD2, S), lambda b, h, i: (h, b))],
        out_specs=pl.BlockSpec((tq, A_HD2), lambda b, h, i: (b * nq + i, h)),
        out_shape=jax.ShapeDtypeStruct((B * S, D_MODEL), BF16),
        compiler_params=_cparams(("arbitrary", "arbitrary", "arbitrary")),
        name="attn_prompt",
    )(*lam_vecs, subln_gcol, q, kb, vbt)


SAMPLE_ROWS = 8
PAGES_PER_STEP = 8


def _attn_sample_kernel(pt_ref, lq1_ref, lk1_ref, lq2_ref, lk2_ref, g_ref, q_ref, kn_ref, vn_ref, *rest,
                        n_steps, dec_seq):
    pp = PAGES_PER_STEP
    k_refs, v_refs = rest[:pp], rest[pp:2 * pp]
    o_ref, qf_ref, bias_ref, m_ref, l_ref, acc_ref = rest[2 * pp:]
    step_id = pl.program_id(1)
    rows = A_HEADS * SAMPLE_ROWS
    flat = PAGE_SIZE * A_HEADS

    @pl.when(step_id == 0)
    def _():
        q8 = q_ref[0]
        qf = jnp.concatenate([q8[:, h * A_HD2:(h + 1) * A_HD2] for h in range(A_HEADS)], axis=0)
        row = lax.broadcasted_iota(jnp.int32, qf.shape, 0)
        lane = lax.broadcasted_iota(jnp.int32, qf.shape, 1)
        first_map = (row % SAMPLE_ROWS) < dec_seq
        qf_ref[...] = jnp.where(first_map == (lane < A_HEAD_DIM), qf, jnp.zeros_like(qf))
        brow = lax.broadcasted_iota(jnp.int32, (rows, flat), 0)
        bcol = lax.broadcasted_iota(jnp.int32, (rows, flat), 1)
        bias_ref[...] = jnp.where((bcol % A_HEADS) == (brow // SAMPLE_ROWS), 0.0, -jnp.inf)
        m_ref[...] = jnp.full(m_ref.shape, -jnp.inf, F32)
        l_ref[...] = jnp.zeros(l_ref.shape, F32)
        acc_ref[...] = jnp.zeros(acc_ref.shape, F32)

    def update(kfs, vfs, bias):
        ss = [_dot_nt(qf_ref[...], kf) + bias for kf in kfs]
        m = m_ref[...]
        m_new = m
        for s in ss:
            m_new = jnp.maximum(m_new, jnp.max(s, axis=1, keepdims=True))
        alpha = jnp.exp(m - m_new)
        l = alpha * l_ref[...]
        acc = alpha * acc_ref[...]
        for s, vf in zip(ss, vfs):
            p = jnp.exp(s - m_new)
            l = l + jnp.sum(p, axis=1, keepdims=True)
            acc = acc + _dot(p.astype(BF16), vf)
        l_ref[...] = l
        acc_ref[...] = acc
        m_ref[...] = m_new

    update([k_refs[t][0, 0].reshape(flat, A_HD2).astype(BF16) for t in range(pp)],
           [v_refs[t][0, 0].reshape(flat, A_HD2).astype(BF16) for t in range(pp)], bias_ref[...])

    @pl.when(step_id == n_steps - 1)
    def _():
        nnew = SAMPLE_ROWS * A_HEADS
        pad = jnp.zeros((PAGE_SIZE - nnew, A_HD2), BF16)
        kf = jnp.concatenate([kn_ref[0].reshape(nnew, A_HD2).astype(BF16), pad], axis=0)
        vf = jnp.concatenate([vn_ref[0].reshape(nnew, A_HD2).astype(BF16), pad], axis=0)
        row = lax.broadcasted_iota(jnp.int32, (rows, PAGE_SIZE), 0)
        col = lax.broadcasted_iota(jnp.int32, (rows, PAGE_SIZE), 1)
        ok = ((col % A_HEADS) == (row // SAMPLE_ROWS)) & ((col // A_HEADS) <= (row % dec_seq))
        update([kf], [vf], jnp.where(ok, 0.0, -jnp.inf))
        lam = _lam_value(lq1_ref[...], lk1_ref[...], lq2_ref[...], lk2_ref[...])
        g = g_ref[...]
        for h in range(A_HEADS):
            rs = slice(h * SAMPLE_ROWS, (h + 1) * SAMPLE_ROWS)
            o = acc_ref[rs, :] / l_ref[rs, :]
            o = o - lam * pltpu.roll(o, SAMPLE_ROWS - dec_seq, 0)
            o_ref[0, :, h * A_HD2:(h + 1) * A_HD2] = _sub_norm(o, g)


def _attn_sample(q8, kn8, vn8, cache_k, cache_v, page_table, lam_vecs, subln_g):
    DB, n_pages = page_table.shape
    pp = PAGES_PER_STEP
    assert n_pages % pp == 0
    n_steps = n_pages // pp
    dec_seq = SAMPLE_ROWS // 2
    rows = A_HEADS * SAMPLE_ROWS
    vec = pl.BlockSpec((1, A_HEAD_DIM), lambda b, s, pt: (0, 0))
    per_b = pl.BlockSpec((1, SAMPLE_ROWS, D_MODEL), lambda b, s, pt: (b, 0, 0))
    new_kv = pl.BlockSpec((1, SAMPLE_ROWS, A_HEADS, A_HD2), lambda b, s, pt: (b, 0, 0, 0))

    def page_spec(t):
        return pl.BlockSpec((1, 1, PAGE_SIZE, A_HEADS, A_HD2),
                            lambda b, s, pt: (0, pt[b, s * pp + t], 0, 0, 0))

    grid_spec = pltpu.PrefetchScalarGridSpec(
        num_scalar_prefetch=1,
        grid=(DB, n_steps),
        in_specs=[vec, vec, vec, vec, pl.BlockSpec((1, A_HD2), lambda b, s, pt: (0, 0)),
                  per_b, new_kv, new_kv]
                 + [page_spec(t) for t in range(pp)] + [page_spec(t) for t in range(pp)],
        out_specs=per_b,
        scratch_shapes=[pltpu.VMEM((rows, A_HD2), BF16),
                        pltpu.VMEM((rows, PAGE_SIZE * A_HEADS), F32),
                        pltpu.VMEM((rows, 1), F32), pltpu.VMEM((rows, 1), F32),
                        pltpu.VMEM((rows, A_HD2), F32)],
    )
    return pl.pallas_call(
        functools.partial(_attn_sample_kernel, n_steps=n_steps, dec_seq=dec_seq),
        grid_spec=grid_spec,
        out_shape=jax.ShapeDtypeStruct((DB, SAMPLE_ROWS, D_MODEL), F32),
        compiler_params=_cparams(("arbitrary", "arbitrary")),
        name="attn_sample",
    )(page_table, *lam_vecs, subln_g, q8, kn8, vn8, *([cache_k] * pp), *([cache_v] * pp))


CONV_HALO = 8
MLSTM_HEADS_PER_STEP = 2


def _log_sigmoid(x):
    return -(jnp.maximum(-x, 0.0) + jnp.log1p(jnp.exp(-jnp.abs(x))))


def _mlstm_kernel(u_ref, halo_ref, vm_ref, om_ref, icol_ref, fcol_ref, irow_ref, frow_ref,
                  wconv_ref, bconv_ref, wq_ref, wk_ref, wkt_ref, g_ref, c0_ref, n0_ref, m0_ref,
                  hb_ref, c_ref, n_ref, m_ref,
                  ext_ref, uc_ref, q_ref, k_ref, ct_ref, nrow_ref, *, S, L):
    kscale = M_HEAD_DIM ** -0.5
    dh = M_HEAD_DIM
    heads = range(MLSTM_HEADS_PER_STEP)
    hcols = [slice(hh * dh, (hh + 1) * dh) for hh in heads]
    ext_ref[0:CONV_HALO, :] = halo_ref[0]
    ext_ref[CONV_HALO:CONV_HALO + S, :] = u_ref[...]
    rb = min(S, 512)
    for r0 in range(0, S, rb):
        acc = jnp.broadcast_to(bconv_ref[...], (rb, ext_ref.shape[1]))
        for jj in range(M_CONV):
            off = CONV_HALO - (M_CONV - 1) + jj + r0
            acc = acc + ext_ref[off:off + rb, :] * wconv_ref[jj:jj + 1, :]
        uc = (acc * jax.nn.sigmoid(acc)).astype(BF16)
        uc_ref[r0:r0 + rb, :] = uc
        for hh in heads:
            q_ref[r0:r0 + rb, hcols[hh]] = _dot(uc[:, hcols[hh]], wq_ref[hh]).astype(BF16)
            k_ref[r0:r0 + rb, hcols[hh]] = (_dot(uc[:, hcols[hh]], wk_ref[hh]) * kscale).astype(BF16)

    for hh in heads:
        ct_ref[hh] = c0_ref[0, hh].T
        nrow_ref[hh] = n0_ref[0, hh]
    ti = lax.broadcasted_iota(jnp.int32, (L, L), 0)
    si = lax.broadcasted_iota(jnp.int32, (L, L), 1)
    tril = si <= ti
    gain = g_ref[...]

    def chunk(c, m_prevs):
        r0 = pl.multiple_of(c * L, L)
        rows = pl.ds(r0, L)
        qcs = [q_ref[rows, hcols[hh]] for hh in heads]
        vcs = [vm_ref[rows, hcols[hh]] for hh in heads]
        kcts = [(_dot_nt(wkt_ref[hh], uc_ref[rows, hcols[hh]]) * kscale).astype(BF16) for hh in heads]
        sqk = [_dot(qcs[hh], kcts[hh]) for hh in heads]
        qct = [_dot(qcs[hh], ct_ref[hh].astype(BF16)) for hh in heads]
        gm = []
        for hh in heads:
            m_prev = m_prevs[hh]
            icol = icol_ref[0, hh, rows, :]
            fcol = _log_sigmoid(fcol_ref[0, hh, rows, :])
            irow = irow_ref[0, hh, pl.ds(c, 1), :]
            frow = _log_sigmoid(frow_ref[0, hh, pl.ds(c, 1), :])
            b_col = jnp.sum(jnp.where(tril, frow, 0.0), axis=1, keepdims=True)
            b_row = jnp.sum(jnp.where(ti <= si, fcol, 0.0), axis=0, keepdims=True)
            dmat = jnp.where(tril, b_col - b_row + irow, -jnp.inf)
            m_t = jnp.maximum(b_col + m_prev, jnp.max(dmat, axis=1, keepdims=True))
            wmat = jnp.exp(dmat - m_t)
            inter = jnp.exp(b_col + m_prev - m_t)
            m_new = m_t[L - 1:L, :]
            b_last = b_col[L - 1:L, :]
            decay = jnp.exp(b_last + m_prev - m_new)
            w_col = jnp.exp(b_last - b_col + icol - m_new)
            gm.append((m_t, wmat, inter, m_new, decay, w_col))
        sws = [sqk[hh] * gm[hh][1] for hh in heads]
        wvs = [(gm[hh][5] * vcs[hh].astype(F32)).astype(BF16) for hh in heads]
        swv = [_dot(sws[hh].astype(BF16), vcs[hh]) for hh in heads]
        upd = [_dot(kcts[hh], wvs[hh]) for hh in heads]
        m_news = []
        for hh in heads:
            m_t, wmat, inter, m_new, decay, w_col = gm[hh]
            qc = qcs[hh]
            kc = k_ref[rows, hcols[hh]]
            num = inter * qct[hh] + swv[hh]
            qn = jnp.sum(qc.astype(F32) * nrow_ref[hh], axis=1, keepdims=True)
            dot = inter * qn + jnp.sum(sws[hh], axis=1, keepdims=True)
            h = num / jnp.maximum(jnp.abs(dot), jnp.exp(-m_t))
            ct_ref[hh] = decay * ct_ref[hh] + upd[hh]
            nrow_ref[hh] = decay * nrow_ref[hh] + jnp.sum(w_col * kc.astype(F32), axis=0, keepdims=True)
            mu = jnp.mean(h, axis=1, keepdims=True)
            hc = h - mu
            var = jnp.mean(hc * hc, axis=1, keepdims=True)
            hn = hc * lax.rsqrt(var + LN_EPS) * gain[:, hcols[hh]]
            hb_ref[rows, hcols[hh]] = (jax.nn.sigmoid(om_ref[rows, hcols[hh]]) * hn).astype(BF16)
            m_news.append(m_new)
        return tuple(m_news)

    m_fin = lax.fori_loop(0, S // L, chunk, tuple(m0_ref[0, hh][:, 0:1] for hh in heads))
    for hh in heads:
        c_ref[0, hh] = ct_ref[hh].T
        n_ref[0, hh] = nrow_ref[hh]
        m_ref[0, hh] = jnp.broadcast_to(m_fin[hh], (1, LANES))


def _mlstm(u, halo, vm, om, icol, fcol, irow, frow, w_conv, b_conv, wq, wk, wkt, mnorm_g,
           c0, n0, m0, B, S):
    L = M_CHUNK
    nc = S // L
    H, dh = M_HEADS, M_HEAD_DIM
    hp = MLSTM_HEADS_PER_STEP
    assert H % hp == 0
    seq = pl.BlockSpec((S, hp * dh), lambda b, h: (b, h))
    col = pl.BlockSpec((1, hp, S, 1), lambda b, h: (b, h, 0, 0))
    rowg = pl.BlockSpec((1, hp, nc, L), lambda b, h: (b, h, 0, 0))
    wmat = pl.BlockSpec((hp, dh, dh), lambda b, h: (h, 0, 0))
    state = pl.BlockSpec((1, hp, dh, dh), lambda b, h: (b, h, 0, 0))
    nvec = pl.BlockSpec((1, hp, 1, dh), lambda b, h: (b, h, 0, 0))
    mvec = pl.BlockSpec((1, hp, 1, LANES), lambda b, h: (b, h, 0, 0))
    return pl.pallas_call(
        functools.partial(_mlstm_kernel, S=S, L=L),
        grid=(B, H // hp),
        in_specs=[seq,
                  pl.BlockSpec((1, CONV_HALO, hp * dh), lambda b, h: (b, 0, h)),
                  seq, seq, col, col, rowg, rowg,
                  pl.BlockSpec((M_CONV, hp * dh), lambda b, h: (0, h)),
                  pl.BlockSpec((1, hp * dh), lambda b, h: (0, h)),
                  wmat, wmat, wmat,
                  pl.BlockSpec((1, hp * dh), lambda b, h: (0, h)),
                  state, nvec, mvec],
        out_specs=[seq, state, nvec, mvec],
        out_shape=[jax.ShapeDtypeStruct((B * S, D_MODEL), BF16),
                   jax.ShapeDtypeStruct((B, H, dh, dh), F32),
                   jax.ShapeDtypeStruct((B, H, 1, dh), F32),
                   jax.ShapeDtypeStruct((B, H, 1, LANES), F32)],
        scratch_shapes=[pltpu.VMEM((S + CONV_HALO, hp * dh), F32),
                        pltpu.VMEM((S, hp * dh), BF16), pltpu.VMEM((S, hp * dh), BF16),
                        pltpu.VMEM((S, hp * dh), BF16),
                        pltpu.VMEM((hp, dh, dh), F32), pltpu.VMEM((hp, 1, dh), F32)],
        compiler_params=_cparams(("arbitrary", "arbitrary")),
        name="mlstm",
    )(u, halo, vm, om, icol, fcol, irow, frow, w_conv, b_conv, wq, wk, wkt, mnorm_g, c0, n0, m0)


def _layer_norm_rows(z, g, b):
    mu = jnp.mean(z, axis=1, keepdims=True)
    zc = z - mu
    var = jnp.mean(zc * zc, axis=1, keepdims=True)
    return zc * lax.rsqrt(var + LN_EPS) * g + b


def _merge_kernel(x_ref, oa_ref, hb_ref, ga_ref, gb_ref, wa_ref, wb_ref, wo_ref, g_ref, b_ref,
                  h1t_ref, h1tb_ref):
    ya = _dot(oa_ref[...], wa_ref[...])
    yb = _dot(hb_ref[...], wb_ref[...])
    merged = jax.nn.sigmoid(ga_ref[...]) * ya + jax.nn.sigmoid(gb_ref[...]) * yb
    out = _dot(merged.astype(BF16), wo_ref[...])
    h1 = _layer_norm_rows(ALPHA * x_ref[...] + out, g_ref[...], b_ref[...])
    h1t = h1.T
    h1t_ref[...] = h1t
    h1tb_ref[...] = h1t.astype(BF16)


def _merge(x, oa, hb, ga, gb, wa, wb, wo, ln_g, ln_b):
    T = x.shape[0]
    tm = min(T, 512)
    tok = pl.BlockSpec((tm, D_MODEL), lambda i: (i, 0))
    wsp = pl.BlockSpec((D_MODEL, D_MODEL), lambda i: (0, 0))
    vsp = pl.BlockSpec((1, D_MODEL), lambda i: (0, 0))
    tsp = pl.BlockSpec((D_MODEL, tm), lambda i: (0, i))
    return pl.pallas_call(
        _merge_kernel,
        grid=(T // tm,),
        in_specs=[tok, tok, tok, tok, tok, wsp, wsp, wsp, vsp, vsp],
        out_specs=[tsp, tsp],
        out_shape=[jax.ShapeDtypeStruct((D_MODEL, T), F32), jax.ShapeDtypeStruct((D_MODEL, T), BF16)],
        compiler_params=_cparams(("arbitrary",)),
        name="merge_ln1",
    )(x, oa, hb, ga, gb, wa, wb, wo, ln_g, ln_b)


NOT_SELECTED = 255.0


def _top16_ranks(s, iota):
    rank = jnp.full(s.shape, NOT_SELECTED, F32)
    tops = []
    for r in range(P_TOPK):
        mx = jnp.max(s, axis=0, keepdims=True)
        idx = jnp.min(jnp.where(s == mx, iota, float(P_NKEYS)), axis=0, keepdims=True)
        sel = iota == idx
        rank = jnp.where(sel, float(r), rank)
        s = jnp.where(sel, -jnp.inf, s)
        tops.append(mx)
    return rank, jnp.concatenate(tops, axis=0)


def _top16(s, iota):
    rank = jnp.full(s.shape, NOT_SELECTED, F32)
    tops = []
    for r in range(P_TOPK):
        mx = jnp.max(s, axis=0, keepdims=True)
        eq = s == mx
        rank = jnp.where(eq, float(r), rank)
        s = jnp.where(eq, -jnp.inf, s)
        tops.append(mx)
    n_ranked = jnp.sum(jnp.where(rank < float(P_TOPK), 1.0, 0.0), axis=0, keepdims=True)
    return rank, jnp.concatenate(tops, axis=0), n_ranked


def _pair_merge(sv1, sv2, riota):
    cnt = jnp.zeros(sv1.shape, F32)
    front = sv1 + sv2[0:1, :]
    top0 = front[0:1, :]
    z = jnp.zeros_like(top0)
    for _ in range(P_TOPK):
        fm = jnp.max(front, axis=0, keepdims=True)
        idx = jnp.min(jnp.where(front == fm, riota, float(P_TOPK)), axis=0, keepdims=True)
        sel = riota == idx
        z = z + jnp.exp(fm - top0)
        cnt = cnt + jnp.where(sel, 1.0, 0.0)
        csel = jnp.sum(jnp.where(sel, cnt, 0.0), axis=0, keepdims=True)
        v1 = jnp.sum(jnp.where(sel, sv1, 0.0), axis=0, keepdims=True)
        v2 = jnp.sum(jnp.where(riota == csel, sv2, 0.0), axis=0, keepdims=True)
        nxt = jnp.where(csel < float(P_TOPK), v1 + v2, -jnp.inf)
        front = jnp.where(sel, nxt, front)
    return cnt, z


def _route_kernel(h1t_ref, wpqt_ref, keys_ref, e1_ref, lf_ref, e2_ref, r2_ref, qt_ref, *, tt):
    qt_ref[...] = _dot(wpqt_ref[...], h1t_ref[...]).astype(BF16)
    kiota = lax.broadcasted_iota(jnp.int32, (P_NKEYS, LANES), 0).astype(F32)
    riota = lax.broadcasted_iota(jnp.int32, (P_TOPK, LANES), 0).astype(F32)

    def head(h, carry):
        r1 = pl.multiple_of(h * 2 * P_HALF_DIM, P_HALF_DIM)
        r2 = pl.multiple_of(h * 2 * P_HALF_DIM + P_HALF_DIM, P_HALF_DIM)
        slabs = [slice(w0, w0 + LANES) for w0 in range(0, tt, LANES)]
        scores = []
        for ws in slabs:
            scores.append(_dot(keys_ref[2 * h], qt_ref[pl.ds(r1, P_HALF_DIM), ws]))
            scores.append(_dot(keys_ref[2 * h + 1], qt_ref[pl.ds(r2, P_HALF_DIM), ws]))
        quick = [_top16(s, kiota) for s in scores]
        n_ranked = quick[0][2]
        for q in quick[1:]:
            n_ranked = jnp.maximum(n_ranked, q[2])
        tied = jnp.max(n_ranked) > float(P_TOPK)
        ranked = lax.cond(tied,
                          lambda: tuple(_top16_ranks(s, kiota) for s in scores),
                          lambda: tuple((q[0], q[1]) for q in quick))
        for si, ws in enumerate(slabs):
            s1, s2 = scores[2 * si], scores[2 * si + 1]
            (rank1, sv1), (rank2, sv2) = ranked[2 * si], ranked[2 * si + 1]
            cnt, z = _pair_merge(sv1, sv2, riota)
            in1 = rank1 < float(P_TOPK)
            in2 = rank2 < float(P_TOPK)
            e1 = jnp.where(in1, jnp.exp(s1 - sv1[0:1, :]) / z, 0.0)
            e2 = jnp.where(in2, jnp.exp(s2 - sv2[0:1, :]), 0.0)
            lf = jnp.zeros_like(s1)
            for r in range(P_TOPK):
                lf = jnp.where(rank1 == float(r), cnt[r:r + 1, :], lf)
            e1_ref[h, :, ws] = e1
            lf_ref[h, :, ws] = lf
            e2_ref[h, :, ws] = e2.astype(BF16)
            r2_ref[h, :, ws] = rank2.astype(BF16)
        return carry

    lax.fori_loop(0, P_HEADS, head, 0)


def _route(h1tb, wpqt, keys):
    T = h1tb.shape[1]
    tt = min(T, 512)
    osp = pl.BlockSpec((P_HEADS, P_NKEYS, tt), lambda i: (0, 0, i))
    of32 = jax.ShapeDtypeStruct((P_HEADS, P_NKEYS, T), F32)
    obf = jax.ShapeDtypeStruct((P_HEADS, P_NKEYS, T), BF16)
    return pl.pallas_call(
        functools.partial(_route_kernel, tt=tt),
        grid=(T // tt,),
        in_specs=[pl.BlockSpec((D_MODEL, tt), lambda i: (0, i)),
                  pl.BlockSpec((2 * P_HEADS * P_HALF_DIM, D_MODEL), lambda i: (0, 0)),
                  pl.BlockSpec((2 * P_HEADS, P_NKEYS, P_HALF_DIM), lambda i: (0, 0, 0))],
        out_specs=[osp, osp, osp, osp],
        out_shape=[of32, of32, obf, obf],
        scratch_shapes=[pltpu.VMEM((2 * P_HEADS * P_HALF_DIM, tt), BF16)],
        compiler_params=_cparams(("arbitrary",)),
        name="peer_route",
    )(h1tb, wpqt, keys)


PEER_EXPERT_BLOCK = 1024
PEER_SUB_BLOCK = 256


def _peer_kernel(h1t_ref, h1tb_ref, e1_ref, lf_ref, e2_ref, r2_ref, pu_ref, pvt_ref, g_ref, b_ref,
                 y_ref, acc_ref, w_ref, *, n_eb):
    eb = pl.program_id(1)
    slot = eb % 2
    rows_per_block = PEER_EXPERT_BLOCK // P_NKEYS
    sub = PEER_SUB_BLOCK
    n_sub = PEER_EXPERT_BLOCK // sub
    tt = acc_ref.shape[1]
    n_parts = 2 if tt % (2 * LANES) == 0 else 1
    pw = tt // n_parts

    def scores(sb):
        return _dot(pu_ref[sb * sub:(sb + 1) * sub, :], h1tb_ref[...])

    def out_part(p):
        return _dot(pvt_ref[...], w_ref[1 - slot, :, p * pw:(p + 1) * pw])

    def gated(a, sb):
        act = (0.5 * a * (1.0 + lax.erf(a * (2.0 ** -0.5)))).astype(BF16)
        gates = []
        for ci in range(sub // P_NKEYS):
            c = eb * rows_per_block + sb * (sub // P_NKEYS) + ci
            gate = jnp.zeros((P_NKEYS, tt), BF16)
            for h in range(P_HEADS):
                lrow = lf_ref[h, pl.ds(c, 1), :].astype(BF16)
                erow = e1_ref[h, pl.ds(c, 1), :].astype(BF16)
                picked = jnp.where(r2_ref[h] < lrow, e2_ref[h], jnp.zeros_like(gate))
                gate = gate + picked * erow
            gates.append(gate)
        w_ref[slot, sb * sub:(sb + 1) * sub, :] = jnp.concatenate(gates, axis=0) * act

    def build(with_out):
        a = [scores(0), scores(1)] + [None] * (n_sub - 2)
        outs = []
        for sb in range(n_sub):
            if with_out and sb < n_parts:
                outs.append(out_part(sb))
            gated(a[sb], sb)
            if sb + 2 < n_sub:
                a[sb + 2] = scores(sb + 2)
        for p, o in enumerate(outs):
            acc_ref[:, p * pw:(p + 1) * pw] += o

    @pl.when(eb == 0)
    def _():
        acc_ref[...] = jnp.zeros(acc_ref.shape, F32)
        build(False)

    @pl.when((eb > 0) & (eb < n_eb))
    def _():
        build(True)

    @pl.when(eb == n_eb)
    def _():
        out = jnp.concatenate([out_part(p) for p in range(n_parts)], axis=1)
        z = (ALPHA * h1t_ref[...] + acc_ref[...] + out).T
        y_ref[...] = _layer_norm_rows(z, g_ref[...], b_ref[...])


def _peer(h1t, h1tb, e1, lf, e2, r2, pu, pvt, ln_g, ln_b):
    T = h1t.shape[1]
    tt = min(T, 512)
    n_eb = P_EXPERTS // PEER_EXPERT_BLOCK
    assert PEER_EXPERT_BLOCK // PEER_SUB_BLOCK >= 2
    tsp = pl.BlockSpec((D_MODEL, tt), lambda i, e: (0, i))
    rsp = pl.BlockSpec((P_HEADS, P_NKEYS, tt), lambda i, e: (0, 0, i))
    vsp = pl.BlockSpec((1, D_MODEL), lambda i, e: (0, 0))
    return pl.pallas_call(
        functools.partial(_peer_kernel, n_eb=n_eb),
        grid=(T // tt, n_eb + 1),
        in_specs=[tsp, tsp, rsp, rsp, rsp, rsp,
                  pl.BlockSpec((PEER_EXPERT_BLOCK, D_MODEL), lambda i, e: (jnp.minimum(e, n_eb - 1), 0)),
                  pl.BlockSpec((D_MODEL, PEER_EXPERT_BLOCK), lambda i, e: (0, jnp.maximum(e - 1, 0))),
                  vsp, vsp],
        out_specs=pl.BlockSpec((tt, D_MODEL), lambda i, e: (i, 0)),
        out_shape=jax.ShapeDtypeStruct((T, D_MODEL), F32),
        scratch_shapes=[pltpu.VMEM((D_MODEL, tt), F32), pltpu.VMEM((2, PEER_EXPERT_BLOCK, tt), BF16)],
        compiler_params=_cparams(("arbitrary", "arbitrary")),
        name="peer_dense",
    )(h1t, h1tb, e1, lf, e2, r2, pu, pvt, ln_g, ln_b)


def _gate_layouts(g, B, S_real, S_pad):
    H, L = M_HEADS, M_CHUNK
    gi = g[:, :H].reshape(B, S_real, H)
    gf = g[:, H:2 * H].reshape(B, S_real, H)
    if S_pad != S_real:
        pw = ((0, 0), (0, S_pad - S_real), (0, 0))
        gi = jnp.pad(gi, pw, constant_values=PAD_LOG_INPUT_GATE)
        gf = jnp.pad(gf, pw, constant_values=-PAD_LOG_INPUT_GATE)
    gi = jnp.swapaxes(gi, 1, 2)
    gf = jnp.swapaxes(gf, 1, 2)
    col = lambda a: a.reshape(B, H, S_pad, 1)
    row = lambda a: a.reshape(B, H, S_pad // L, L)
    return col(gi), col(gf), row(gi), row(gf)


def _pad_seq(a, B, S_real, S_pad):
    if S_pad == S_real:
        return a
    a = a.reshape(B, S_real, a.shape[-1])
    return jnp.pad(a, ((0, 0), (0, S_pad - S_real), (0, 0))).reshape(B * S_pad, a.shape[-1])


def _group(x3, attend, state, W):
    B, S, _ = x3.shape
    T = B * S
    x = x3.reshape(T, D_MODEL)
    q, k, kb, v, u, vm, om, ga, gb, g, vbt = _in_proj(x, W["w_main"], W["b_main"], W["w_gate"],
                                                      W["b_gate"])
    oa = attend(q, k, v, kb, vbt)

    c0, n0, m0, conv0 = state
    S_pad = -(-S // M_CHUNK) * M_CHUNK
    icol, fcol, irow, frow = _gate_layouts(g, B, S, S_pad)
    halo = jnp.pad(conv0, ((0, 0), (CONV_HALO - (M_CONV - 1), 0), (0, 0)))
    hb, c_new, n_new, m_new = _mlstm(
        _pad_seq(u, B, S, S_pad), halo, _pad_seq(vm, B, S, S_pad), _pad_seq(om, B, S, S_pad),
        icol, fcol, irow, frow, W["w_conv"], W["b_conv"], W["w_qm"], W["w_km"], W["w_kmt"], W["mnorm_g"],
        c0, n0.reshape(B, M_HEADS, 1, M_HEAD_DIM),
        jnp.broadcast_to(m0[:, :, None, None], (B, M_HEADS, 1, LANES)), B, S_pad)
    if S_pad != S:
        hb = hb.reshape(B, S_pad, D_MODEL)[:, :S].reshape(T, D_MODEL)
    u3 = u.reshape(B, S, D_MODEL)
    conv_new = jnp.concatenate([conv0, u3], axis=1)[:, S:]

    h1t, h1tb = _merge(x, oa, hb, ga, gb, W["w_a"], W["w_b"], W["w_o"], W["ln1_g"], W["ln1_b"])
    e1, lf, e2, r2 = _route(h1tb, W["w_pqt"], W["p_keys"])
    y = _peer(h1t, h1tb, e1, lf, e2, r2, W["p_u"], W["p_vt"], W["ln2_g"], W["ln2_b"])

    new_state = (k.reshape(1, B, S, A_HEADS, A_HD2), v.reshape(1, B, S, A_HEADS, A_HD2),
                 c_new[None], n_new.reshape(1, B, M_HEADS, M_HEAD_DIM), m_new[None, :, :, 0, 0],
                 conv_new[None])
    return y.reshape(B, S, D_MODEL), new_state


def kernel(x_prompt, x_sample, cache_k, cache_v, state_C, state_n, state_m, state_conv, page_table,
           w_in, b_in, lam_q1, lam_k1, lam_q2, lam_k2, subln_g, w_conv, b_conv, w_qm, w_km, mnorm_g,
           w_a, w_b, w_o, ln1_g, ln1_b, w_pq, p_keys, p_u, p_v, ln2_g, ln2_b):
    assert w_in.shape[0] == DEPTH
    B, S, _ = x_prompt.shape
    DB, DS, _ = x_sample.shape
    assert 2 * DS == SAMPLE_ROWS

    n_pre = 6 * D_MODEL
    wi, bi = w_in[0], b_in[0]
    W = {
        "w_main": jnp.concatenate([wi[:, :n_pre], wi[:, n_pre + GATE_COLS:]], axis=1).astype(BF16),
        "b_main": jnp.concatenate([bi[:n_pre], bi[n_pre + GATE_COLS:]])[None],
        "w_gate": jnp.pad(wi[:, n_pre:n_pre + GATE_COLS], ((0, 0), (0, LANES - GATE_COLS))).astype(BF16),
        "b_gate": jnp.pad(bi[n_pre:n_pre + GATE_COLS], (0, LANES - GATE_COLS))[None],
        "w_conv": w_conv[0], "b_conv": b_conv[0][None],
        "w_qm": w_qm[0].astype(BF16), "w_km": w_km[0].astype(BF16),
        "w_kmt": jnp.swapaxes(w_km[0], 1, 2).astype(BF16),
        "mnorm_g": mnorm_g[0][None],
        "w_a": w_a[0].astype(BF16), "w_b": w_b[0].astype(BF16), "w_o": w_o[0].astype(BF16),
        "ln1_g": ln1_g[0][None], "ln1_b": ln1_b[0][None],
        "w_pqt": w_pq[0].T.astype(BF16),
        "p_keys": p_keys[0].reshape(2 * P_HEADS, P_NKEYS, P_HALF_DIM).astype(BF16),
        "p_u": p_u[0].astype(BF16), "p_vt": p_v[0].T.astype(BF16),
        "ln2_g": ln2_g[0][None], "ln2_b": ln2_b[0][None],
    }
    lam_vecs = (lam_q1[0][None], lam_k1[0][None], lam_q2[0][None], lam_k2[0][None])
    sg = subln_g[0][None]

    def attend_prompt(q, k, v, kb, vbt):
        return _attn_prompt(q, kb, vbt, lam_vecs, subln_g[0][:, None], B, S)

    def attend_sample(q, k, v, kb, vbt):
        q3 = q.reshape(DB, DS, D_MODEL)
        q8 = jnp.concatenate([q3, q3], axis=1)
        padn = ((0, 0), (0, SAMPLE_ROWS - DS), (0, 0), (0, 0))
        kn8 = jnp.pad(k.reshape(DB, DS, A_HEADS, A_HD2), padn)
        vn8 = jnp.pad(v.reshape(DB, DS, A_HEADS, A_HD2), padn)
        o8 = _attn_sample(q8, kn8, vn8, cache_k, cache_v, page_table, lam_vecs, sg)
        return o8[:, :DS].reshape(DB * DS, D_MODEL).astype(BF16)

    zero_state = (jnp.zeros((B, M_HEADS, M_HEAD_DIM, M_HEAD_DIM), F32),
                  jnp.zeros((B, M_HEADS, M_HEAD_DIM), F32),
                  jnp.zeros((B, M_HEADS), F32),
                  jnp.zeros((B, M_CONV - 1, D_MODEL), F32))
    yp, sp = _group(x_prompt, attend_prompt, zero_state, W)
    ys, ss = _group(x_sample, attend_sample,
                    (state_C[0], state_n[0], state_m[0], state_conv[0]), W)
    return (yp, ys) + sp + ss
```

```python
import functools
import math

import jax
import jax.numpy as jnp
from jax import lax
from jax.experimental import pallas as pl
from jax.experimental.pallas import tpu as pltpu

F32 = jnp.float32
BF16 = jnp.bfloat16

D_MODEL = 1024
A_HEADS = 8
A_HEAD_DIM = 64
A_HD2 = 2 * A_HEAD_DIM
M_HEADS = 4
M_HEAD_DIM = 256
M_CONV = 4
M_CHUNK = 64
P_HEADS = 8
P_NKEYS = 128
P_HALF_DIM = 128
P_TOPK = 16
P_EXPERTS = P_NKEYS * P_NKEYS
PAGE_SIZE = 128
DEPTH = 1
ALPHA = (2.0 * DEPTH) ** 0.25
LN_EPS = 1e-5
PAD_LOG_INPUT_GATE = -1e30
LAM_INIT = 0.8 - 0.6 * math.exp(-0.3 * 0)

N_MAIN_GROUPS = 8
GATE_COLS = 2 * M_HEADS
LANES = 128
VMEM_LIMIT = 56 * 1024 * 1024

NT_DIMS = (((1,), (1,)), ((), ()))


def _cparams(sem):
    return pltpu.CompilerParams(dimension_semantics=sem, vmem_limit_bytes=VMEM_LIMIT)


def _dot(a, b):
    return jnp.dot(a, b, preferred_element_type=F32)


def _dot_nt(a, b):
    return lax.dot_general(a, b, NT_DIMS, preferred_element_type=F32)


def _in_proj_kernel(x_ref, w_ref, b_ref, wg_ref, bg_ref,
                    q_ref, k_ref, kb_ref, v_ref, u_ref, vm_ref, om_ref, ga_ref, gb_ref,
                    g_ref, vbt_ref):
    xb = x_ref[...].astype(BF16)

    def proj(j):
        cols = slice(j * D_MODEL, (j + 1) * D_MODEL)
        return _dot(xb, w_ref[:, cols]) + b_ref[:, cols]

    q_ref[...] = (proj(0) * (A_HEAD_DIM ** -0.5)).astype(BF16)
    g_ref[...] = _dot(xb, wg_ref[...]) + bg_ref[...]
    k_ref[...] = proj(1)
    kb_ref[...] = k_ref[...].astype(BF16)
    v_ref[...] = proj(2)
    vbt_ref[...] = v_ref[...].T.astype(BF16)
    u_ref[...] = proj(3)
    vm_ref[...] = proj(4).astype(BF16)
    om_ref[...] = proj(5)
    ga_ref[...] = proj(6)
    gb_ref[...] = proj(7)


IN_PROJ_ROWS = 256


def _in_proj(x, w_main, b_main, w_gate, b_gate):
    T = x.shape[0]
    tm = min(T, IN_PROJ_ROWS)
    assert T % tm == 0
    row = lambda i: (i, 0)
    whole = lambda i: (0, 0)
    f32o = jax.ShapeDtypeStruct((T, D_MODEL), F32)
    bf16o = jax.ShapeDtypeStruct((T, D_MODEL), BF16)
    ncol = N_MAIN_GROUPS * D_MODEL
    return pl.pallas_call(
        _in_proj_kernel,
        grid=(T // tm,),
        in_specs=[
            pl.BlockSpec((tm, D_MODEL), row),
            pl.BlockSpec((D_MODEL, ncol), whole),
            pl.BlockSpec((1, ncol), whole),
            pl.BlockSpec((D_MODEL, LANES), whole),
            pl.BlockSpec((1, LANES), whole),
        ],
        out_specs=[pl.BlockSpec((tm, D_MODEL), row)] * 9 + [pl.BlockSpec((tm, LANES), row),
                                                           pl.BlockSpec((D_MODEL, tm), lambda i: (0, i))],
        out_shape=[bf16o, f32o, bf16o, f32o, f32o, bf16o, f32o, f32o, f32o,
                   jax.ShapeDtypeStruct((T, LANES), F32), jax.ShapeDtypeStruct((D_MODEL, T), BF16)],
        compiler_params=_cparams(("arbitrary",)),
        name="in_proj",
    )(x, w_main, b_main, w_gate, b_gate)


def _lam_value(lq1, lk1, lq2, lk2):
    a = jnp.exp(jnp.sum(lq1 * lk1, axis=1, keepdims=True))
    b = jnp.exp(jnp.sum(lq2 * lk2, axis=1, keepdims=True))
    return a - b + LAM_INIT


def _sub_norm(o, g):
    r = lax.rsqrt(jnp.mean(o * o, axis=1, keepdims=True) + LN_EPS)
    return o * r * g * (1.0 - LAM_INIT)


ATTN_TQ = 256


def _attn_prompt_kernel(lq1_ref, lk1_ref, lq2_ref, lk2_ref, gcol_ref, q_ref, k_ref, vt_ref, o_ref, *, nq):
    i = pl.program_id(2)
    tq = ATTN_TQ
    lam = _lam_value(lq1_ref[...], lk1_ref[...], lq2_ref[...], lk2_ref[...])
    krow = lax.broadcasted_iota(jnp.int32, (tq, tq), 0)
    qcol = lax.broadcasted_iota(jnp.int32, (tq, tq), 1)
    hcols = [slice(hh * A_HD2, (hh + 1) * A_HD2) for hh in range(ATTN_HEADS_PER_STEP)]

    def query_maps(hh):
        q = q_ref[:, hcols[hh]]
        lane = lax.broadcasted_iota(jnp.int32, q.shape, 1)
        zero = jnp.zeros_like(q)
        return jnp.where(lane < A_HEAD_DIM, q, zero), jnp.where(lane >= A_HEAD_DIM, q, zero)

    def scores(qm, n, hh):
        lo, hi = n * tq, (n + 1) * tq
        sd = _dot_nt(k_ref[lo:hi, hcols[hh]], qm)
        st = _dot_nt(k_ref[0:lo, hcols[hh]], qm) if n else None
        return sd, st

    def attend(sd, st, n, hh):
        lo, hi = n * tq, (n + 1) * tq
        sd = jnp.where(krow <= qcol, sd, -jnp.inf)
        m = jnp.max(sd, axis=0, keepdims=True)
        if n:
            m = jnp.maximum(m, jnp.max(st, axis=0, keepdims=True))
        pd = jnp.exp(sd - m)
        l = jnp.sum(pd, axis=0, keepdims=True)
        acc = _dot(vt_ref[hcols[hh], lo:hi], pd.astype(BF16))
        if n:
            pt = jnp.exp(st - m)
            l = l + jnp.sum(pt, axis=0, keepdims=True)
            acc = acc + _dot(vt_ref[hcols[hh], 0:lo], pt.astype(BF16))
        return acc / l

    for n in range(nq):
        @pl.when(i == n)
        def _(n=n):
            maps = [query_maps(hh) for hh in range(ATTN_HEADS_PER_STEP)]
            sc = [(scores(m1, n, hh), scores(m2, n, hh)) for hh, (m1, m2) in enumerate(maps)]
            for hh, (s1, s2) in enumerate(sc):
                o = attend(*s1, n, hh) - lam * attend(*s2, n, hh)
                r = lax.rsqrt(jnp.mean(o * o, axis=0, keepdims=True) + LN_EPS)
                o = o * r * gcol_ref[...] * (1.0 - LAM_INIT)
                o_ref[:, hcols[hh]] = o.T.astype(BF16)


ATTN_HEADS_PER_STEP = 2


def _attn_prompt(q, kb, vbt, lam_vecs, subln_gcol, B, S):
    tq = ATTN_TQ
    assert S % tq == 0 and A_HEADS % ATTN_HEADS_PER_STEP == 0
    nq = S // tq
    hw = ATTN_HEADS_PER_STEP * A_HD2
    vec = pl.BlockSpec((1, A_HEAD_DIM), lambda b, h, i: (0, 0))
    return pl.pallas_call(
        functools.partial(_attn_prompt_kernel, nq=nq),
        grid=(B, A_HEADS // ATTN_HEADS_PER_STEP, nq),
        in_specs=[vec, vec, vec, vec,
                  pl.BlockSpec((A_HD2, 1), lambda b, h, i: (0, 0)),
                  pl.BlockSpec((tq, hw), lambda b, h, i: (b * nq + i, h)),
                  pl.BlockSpec((S, hw), lambda b, h, i: (b, h)),
                  pl.BlockSpec((hw, S), lambda b, h, i: (h, b))],
        out_specs=pl.BlockSpec((tq, hw), lambda b, h, i: (b * nq + i, h)),
        out_shape=jax.ShapeDtypeStruct((B * S, D_MODEL), BF16),
        compiler_params=_cparams(("arbitrary", "arbitrary", "arbitrary")),
        name="attn_prompt",
    )(*lam_vecs, subln_gcol, q, kb, vbt)


SAMPLE_ROWS = 8
PAGES_PER_STEP = 8


def _attn_sample_kernel(pt_ref, lq1_ref, lk1_ref, lq2_ref, lk2_ref, g_ref, q_ref, kn_ref, vn_ref, *rest,
                        n_steps, dec_seq):
    pp = PAGES_PER_STEP
    k_refs, v_refs = rest[:pp], rest[pp:2 * pp]
    o_ref, qf_ref, bias_ref, m_ref, l_ref, acc_ref = rest[2 * pp:]
    step_id = pl.program_id(1)
    rows = A_HEADS * SAMPLE_ROWS
    flat = PAGE_SIZE * A_HEADS

    @pl.when(step_id == 0)
    def _():
        q8 = q_ref[0]
        qf = jnp.concatenate([q8[:, h * A_HD2:(h + 1) * A_HD2] for h in range(A_HEADS)], axis=0)
        row = lax.broadcasted_iota(jnp.int32, qf.shape, 0)
        lane = lax.broadcasted_iota(jnp.int32, qf.shape, 1)
        first_map = (row % SAMPLE_ROWS) < dec_seq
        qf_ref[...] = jnp.where(first_map == (lane < A_HEAD_DIM), qf, jnp.zeros_like(qf))
        brow = lax.broadcasted_iota(jnp.int32, (rows, flat), 0)
        bcol = lax.broadcasted_iota(jnp.int32, (rows, flat), 1)
        bias_ref[...] = jnp.where((bcol % A_HEADS) == (brow // SAMPLE_ROWS), 0.0, -jnp.inf)
        m_ref[...] = jnp.full(m_ref.shape, -jnp.inf, F32)
        l_ref[...] = jnp.zeros(l_ref.shape, F32)
        acc_ref[...] = jnp.zeros(acc_ref.shape, F32)

    def update(kfs, vfs, bias):
        ss = [_dot_nt(qf_ref[...], kf) + bias for kf in kfs]
        m = m_ref[...]
        m_new = m
        for s in ss:
            m_new = jnp.maximum(m_new, jnp.max(s, axis=1, keepdims=True))
        alpha = jnp.exp(m - m_new)
        l = alpha * l_ref[...]
        acc = alpha * acc_ref[...]
        for s, vf in zip(ss, vfs):
            p = jnp.exp(s - m_new)
            l = l + jnp.sum(p, axis=1, keepdims=True)
            acc = acc + _dot(p.astype(BF16), vf)
        l_ref[...] = l
        acc_ref[...] = acc
        m_ref[...] = m_new

    update([k_refs[t][0, 0].reshape(flat, A_HD2).astype(BF16) for t in range(pp)],
           [v_refs[t][0, 0].reshape(flat, A_HD2).astype(BF16) for t in range(pp)], bias_ref[...])

    @pl.when(step_id == n_steps - 1)
    def _():
        nnew = SAMPLE_ROWS * A_HEADS
        pad = jnp.zeros((PAGE_SIZE - nnew, A_HD2), BF16)
        kf = jnp.concatenate([kn_ref[0].reshape(nnew, A_HD2).astype(BF16), pad], axis=0)
        vf = jnp.concatenate([vn_ref[0].reshape(nnew, A_HD2).astype(BF16), pad], axis=0)
        row = lax.broadcasted_iota(jnp.int32, (rows, PAGE_SIZE), 0)
        col = lax.broadcasted_iota(jnp.int32, (rows, PAGE_SIZE), 1)
        ok = ((col % A_HEADS) == (row // SAMPLE_ROWS)) & ((col // A_HEADS) <= (row % dec_seq))
        update([kf], [vf], jnp.where(ok, 0.0, -jnp.inf))
        lam = _lam_value(lq1_ref[...], lk1_ref[...], lq2_ref[...], lk2_ref[...])
        g = g_ref[...]
        for h in range(A_HEADS):
            rs = slice(h * SAMPLE_ROWS, (h + 1) * SAMPLE_ROWS)
            o = acc_ref[rs, :] / l_ref[rs, :]
            o = o - lam * pltpu.roll(o, SAMPLE_ROWS - dec_seq, 0)
            o_ref[0, :, h * A_HD2:(h + 1) * A_HD2] = _sub_norm(o, g)


def _attn_sample(q8, kn8, vn8, cache_k, cache_v, page_table, lam_vecs, subln_g):
    DB, n_pages = page_table.shape
    pp = PAGES_PER_STEP
    assert n_pages % pp == 0
    n_steps = n_pages // pp
    dec_seq = SAMPLE_ROWS // 2
    rows = A_HEADS * SAMPLE_ROWS
    vec = pl.BlockSpec((1, A_HEAD_DIM), lambda b, s, pt: (0, 0))
    per_b = pl.BlockSpec((1, SAMPLE_ROWS, D_MODEL), lambda b, s, pt: (b, 0, 0))
    new_kv = pl.BlockSpec((1, SAMPLE_ROWS, A_HEADS, A_HD2), lambda b, s, pt: (b, 0, 0, 0))

    def page_spec(t):
        return pl.BlockSpec((1, 1, PAGE_SIZE, A_HEADS, A_HD2),
                            lambda b, s, pt: (0, pt[b, s * pp + t], 0, 0, 0))

    grid_spec = pltpu.PrefetchScalarGridSpec(
        num_scalar_prefetch=1,
        grid=(DB, n_steps),
        in_specs=[vec, vec, vec, vec, pl.BlockSpec((1, A_HD2), lambda b, s, pt: (0, 0)),
                  per_b, new_kv, new_kv]
                 + [page_spec(t) for t in range(pp)] + [page_spec(t) for t in range(pp)],
        out_specs=per_b,
        scratch_shapes=[pltpu.VMEM((rows, A_HD2), BF16),
                        pltpu.VMEM((rows, PAGE_SIZE * A_HEADS), F32),
                        pltpu.VMEM((rows, 1), F32), pltpu.VMEM((rows, 1), F32),
                        pltpu.VMEM((rows, A_HD2), F32)],
    )
    return pl.pallas_call(
        functools.partial(_attn_sample_kernel, n_steps=n_steps, dec_seq=dec_seq),
        grid_spec=grid_spec,
        out_shape=jax.ShapeDtypeStruct((DB, SAMPLE_ROWS, D_MODEL), F32),
        compiler_params=_cparams(("arbitrary", "arbitrary")),
        name="attn_sample",
    )(page_table, *lam_vecs, subln_g, q8, kn8, vn8, *([cache_k] * pp), *([cache_v] * pp))


CONV_HALO = 8
MLSTM_HEADS_PER_STEP = 2


def _log_sigmoid(x):
    return -(jnp.maximum(-x, 0.0) + jnp.log1p(jnp.exp(-jnp.abs(x))))


def _mlstm_kernel(u_ref, halo_ref, vm_ref, om_ref, icol_ref, fcol_ref, irow_ref, frow_ref,
                  wconv_ref, bconv_ref, wq_ref, wk_ref, wkt_ref, g_ref, c0_ref, n0_ref, m0_ref,
                  hb_ref, c_ref, n_ref, m_ref,
                  ext_ref, uc_ref, q_ref, k_ref, ct_ref, nrow_ref, *, S, L):
    kscale = M_HEAD_DIM ** -0.5
    dh = M_HEAD_DIM
    heads = range(MLSTM_HEADS_PER_STEP)
    hcols = [slice(hh * dh, (hh + 1) * dh) for hh in heads]
    ext_ref[0:CONV_HALO, :] = halo_ref[0]
    ext_ref[CONV_HALO:CONV_HALO + S, :] = u_ref[...]
    rb = min(S, 512)
    for r0 in range(0, S, rb):
        acc = jnp.broadcast_to(bconv_ref[...], (rb, ext_ref.shape[1]))
        for jj in range(M_CONV):
            off = CONV_HALO - (M_CONV - 1) + jj + r0
            acc = acc + ext_ref[off:off + rb, :] * wconv_ref[jj:jj + 1, :]
        uc = (acc * jax.nn.sigmoid(acc)).astype(BF16)
        uc_ref[r0:r0 + rb, :] = uc
        for hh in heads:
            q_ref[r0:r0 + rb, hcols[hh]] = _dot(uc[:, hcols[hh]], wq_ref[hh]).astype(BF16)
            k_ref[r0:r0 + rb, hcols[hh]] = (_dot(uc[:, hcols[hh]], wk_ref[hh]) * kscale).astype(BF16)

    for hh in heads:
        ct_ref[hh] = c0_ref[0, hh].T
        nrow_ref[hh] = n0_ref[0, hh]
    ti = lax.broadcasted_iota(jnp.int32, (L, L), 0)
    si = lax.broadcasted_iota(jnp.int32, (L, L), 1)
    tril = si <= ti
    gain = g_ref[...]

    def chunk(c, m_prevs):
        r0 = pl.multiple_of(c * L, L)
        rows = pl.ds(r0, L)
        qcs = [q_ref[rows, hcols[hh]] for hh in heads]
        vcs = [vm_ref[rows, hcols[hh]] for hh in heads]
        kcts = [(_dot_nt(wkt_ref[hh], uc_ref[rows, hcols[hh]]) * kscale).astype(BF16) for hh in heads]
        sqk = [_dot(qcs[hh], kcts[hh]) for hh in heads]
        qct = [_dot(qcs[hh], ct_ref[hh].astype(BF16)) for hh in heads]
        gm = []
        for hh in heads:
            m_prev = m_prevs[hh]
            icol = icol_ref[0, hh, rows, :]
            fcol = _log_sigmoid(fcol_ref[0, hh, rows, :])
            irow = irow_ref[0, hh, pl.ds(c, 1), :]
            frow = _log_sigmoid(frow_ref[0, hh, pl.ds(c, 1), :])
            b_col = jnp.sum(jnp.where(tril, frow, 0.0), axis=1, keepdims=True)
            b_row = jnp.sum(jnp.where(ti <= si, fcol, 0.0), axis=0, keepdims=True)
            dmat = jnp.where(tril, b_col - b_row + irow, -jnp.inf)
            m_t = jnp.maximum(b_col + m_prev, jnp.max(dmat, axis=1, keepdims=True))
            wmat = jnp.exp(dmat - m_t)
            inter = jnp.exp(b_col + m_prev - m_t)
            m_new = m_t[L - 1:L, :]
            b_last = b_col[L - 1:L, :]
            decay = jnp.exp(b_last + m_prev - m_new)
            w_col = jnp.exp(b_last - b_col + icol - m_new)
            gm.append((m_t, wmat, inter, m_new, decay, w_col))
        sws = [sqk[hh] * gm[hh][1] for hh in heads]
        wvs = [(gm[hh][5] * vcs[hh].astype(F32)).astype(BF16) for hh in heads]
        swv = [_dot(sws[hh].astype(BF16), vcs[hh]) for hh in heads]
        upd = [_dot(kcts[hh], wvs[hh]) for hh in heads]
        m_news = []
        for hh in heads:
            m_t, wmat, inter, m_new, decay, w_col = gm[hh]
            qc = qcs[hh]
            kc = k_ref[rows, hcols[hh]]
            num = inter * qct[hh] + swv[hh]
            qn = jnp.sum(qc.astype(F32) * nrow_ref[hh], axis=1, keepdims=True)
            dot = inter * qn + jnp.sum(sws[hh], axis=1, keepdims=True)
            h = num / jnp.maximum(jnp.abs(dot), jnp.exp(-m_t))
            ct_ref[hh] = decay * ct_ref[hh] + upd[hh]
            nrow_ref[hh] = decay * nrow_ref[hh] + jnp.sum(w_col * kc.astype(F32), axis=0, keepdims=True)
            mu = jnp.mean(h, axis=1, keepdims=True)
            hc = h - mu
            var = jnp.mean(hc * hc, axis=1, keepdims=True)
            hn = hc * lax.rsqrt(var + LN_EPS) * gain[:, hcols[hh]]
            hb_ref[rows, hcols[hh]] = (jax.nn.sigmoid(om_ref[rows, hcols[hh]]) * hn).astype(BF16)
            m_news.append(m_new)
        return tuple(m_news)

    m_fin = lax.fori_loop(0, S // L, chunk, tuple(m0_ref[0, hh][:, 0:1] for hh in heads))
    for hh in heads:
        c_ref[0, hh] = ct_ref[hh].T
        n_ref[0, hh] = nrow_ref[hh]
        m_ref[0, hh] = jnp.broadcast_to(m_fin[hh], (1, LANES))


def _mlstm(u, halo, vm, om, icol, fcol, irow, frow, w_conv, b_conv, wq, wk, wkt, mnorm_g,
           c0, n0, m0, B, S):
    L = M_CHUNK
    nc = S // L
    H, dh = M_HEADS, M_HEAD_DIM
    hp = MLSTM_HEADS_PER_STEP
    assert H % hp == 0
    seq = pl.BlockSpec((S, hp * dh), lambda b, h: (b, h))
    col = pl.BlockSpec((1, hp, S, 1), lambda b, h: (b, h, 0, 0))
    rowg = pl.BlockSpec((1, hp, nc, L), lambda b, h: (b, h, 0, 0))
    wmat = pl.BlockSpec((hp, dh, dh), lambda b, h: (h, 0, 0))
    state = pl.BlockSpec((1, hp, dh, dh), lambda b, h: (b, h, 0, 0))
    nvec = pl.BlockSpec((1, hp, 1, dh), lambda b, h: (b, h, 0, 0))
    mvec = pl.BlockSpec((1, hp, 1, LANES), lambda b, h: (b, h, 0, 0))
    return pl.pallas_call(
        functools.partial(_mlstm_kernel, S=S, L=L),
        grid=(B, H // hp),
        in_specs=[seq,
                  pl.BlockSpec((1, CONV_HALO, hp * dh), lambda b, h: (b, 0, h)),
                  seq, seq, col, col, rowg, rowg,
                  pl.BlockSpec((M_CONV, hp * dh), lambda b, h: (0, h)),
                  pl.BlockSpec((1, hp * dh), lambda b, h: (0, h)),
                  wmat, wmat, wmat,
                  pl.BlockSpec((1, hp * dh), lambda b, h: (0, h)),
                  state, nvec, mvec],
        out_specs=[seq, state, nvec, mvec],
        out_shape=[jax.ShapeDtypeStruct((B * S, D_MODEL), BF16),
                   jax.ShapeDtypeStruct((B, H, dh, dh), F32),
                   jax.ShapeDtypeStruct((B, H, 1, dh), F32),
                   jax.ShapeDtypeStruct((B, H, 1, LANES), F32)],
        scratch_shapes=[pltpu.VMEM((S + CONV_HALO, hp * dh), F32),
                        pltpu.VMEM((S, hp * dh), BF16), pltpu.VMEM((S, hp * dh), BF16),
                        pltpu.VMEM((S, hp * dh), BF16),
                        pltpu.VMEM((hp, dh, dh), F32), pltpu.VMEM((hp, 1, dh), F32)],
        compiler_params=_cparams(("arbitrary", "arbitrary")),
        name="mlstm",
    )(u, halo, vm, om, icol, fcol, irow, frow, w_conv, b_conv, wq, wk, wkt, mnorm_g, c0, n0, m0)


def _layer_norm_rows(z, g, b):
    mu = jnp.mean(z, axis=1, keepdims=True)
    zc = z - mu
    var = jnp.mean(zc * zc, axis=1, keepdims=True)
    return zc * lax.rsqrt(var + LN_EPS) * g + b


def _merge_kernel(x_ref, oa_ref, hb_ref, ga_ref, gb_ref, wa_ref, wb_ref, wo_ref, g_ref, b_ref,
                  h1t_ref, h1tb_ref):
    ya = _dot(oa_ref[...], wa_ref[...])
    yb = _dot(hb_ref[...], wb_ref[...])
    merged = jax.nn.sigmoid(ga_ref[...]) * ya + jax.nn.sigmoid(gb_ref[...]) * yb
    out = _dot(merged.astype(BF16), wo_ref[...])
    h1 = _layer_norm_rows(ALPHA * x_ref[...] + out, g_ref[...], b_ref[...])
    h1t = h1.T
    h1t_ref[...] = h1t
    h1tb_ref[...] = h1t.astype(BF16)


def _merge(x, oa, hb, ga, gb, wa, wb, wo, ln_g, ln_b):
    T = x.shape[0]
    tm = min(T, 512)
    tok = pl.BlockSpec((tm, D_MODEL), lambda i: (i, 0))
    wsp = pl.BlockSpec((D_MODEL, D_MODEL), lambda i: (0, 0))
    vsp = pl.BlockSpec((1, D_MODEL), lambda i: (0, 0))
    tsp = pl.BlockSpec((D_MODEL, tm), lambda i: (0, i))
    return pl.pallas_call(
        _merge_kernel,
        grid=(T // tm,),
        in_specs=[tok, tok, tok, tok, tok, wsp, wsp, wsp, vsp, vsp],
        out_specs=[tsp, tsp],
        out_shape=[jax.ShapeDtypeStruct((D_MODEL, T), F32), jax.ShapeDtypeStruct((D_MODEL, T), BF16)],
        compiler_params=_cparams(("arbitrary",)),
        name="merge_ln1",
    )(x, oa, hb, ga, gb, wa, wb, wo, ln_g, ln_b)


NOT_SELECTED = 255.0


def _top16_ranks(s, iota):
    rank = jnp.full(s.shape, NOT_SELECTED, F32)
    tops = []
    for r in range(P_TOPK):
        mx = jnp.max(s, axis=0, keepdims=True)
        idx = jnp.min(jnp.where(s == mx, iota, float(P_NKEYS)), axis=0, keepdims=True)
        sel = iota == idx
        rank = jnp.where(sel, float(r), rank)
        s = jnp.where(sel, -jnp.inf, s)
        tops.append(mx)
    return rank, jnp.concatenate(tops, axis=0)


def _top16(s, iota):
    rank = jnp.full(s.shape, NOT_SELECTED, F32)
    tops = []
    for r in range(P_TOPK):
        mx = jnp.max(s, axis=0, keepdims=True)
        eq = s == mx
        rank = jnp.where(eq, float(r), rank)
        s = jnp.where(eq, -jnp.inf, s)
        tops.append(mx)
    n_ranked = jnp.sum(jnp.where(rank < float(P_TOPK), 1.0, 0.0), axis=0, keepdims=True)
    return rank, jnp.concatenate(tops, axis=0), n_ranked


def _pair_merge(sv1, sv2, riota):
    cnt = jnp.zeros(sv1.shape, F32)
    front = sv1 + sv2[0:1, :]
    top0 = front[0:1, :]
    z = jnp.zeros_like(top0)
    for _ in range(P_TOPK):
        fm = jnp.max(front, axis=0, keepdims=True)
        idx = jnp.min(jnp.where(front == fm, riota, float(P_TOPK)), axis=0, keepdims=True)
        sel = riota == idx
        z = z + jnp.exp(fm - top0)
        cnt = cnt + jnp.where(sel, 1.0, 0.0)
        csel = jnp.sum(jnp.where(sel, cnt, 0.0), axis=0, keepdims=True)
        v1 = jnp.sum(jnp.where(sel, sv1, 0.0), axis=0, keepdims=True)
        v2 = jnp.sum(jnp.where(riota == csel, sv2, 0.0), axis=0, keepdims=True)
        nxt = jnp.where(csel < float(P_TOPK), v1 + v2, -jnp.inf)
        front = jnp.where(sel, nxt, front)
    return cnt, z


def _route_kernel(h1t_ref, wpqt_ref, keys_ref, e1_ref, lf_ref, e2_ref, r2_ref, qt_ref, *, tt):
    kiota = lax.broadcasted_iota(jnp.int32, (P_NKEYS, LANES), 0).astype(F32)
    riota = lax.broadcasted_iota(jnp.int32, (P_TOPK, LANES), 0).astype(F32)
    hq = 2 * P_HALF_DIM

    def head_queries(h):
        rows = pl.ds(pl.multiple_of(h * hq, hq), hq)
        return _dot(wpqt_ref[rows, :], h1t_ref[...]).astype(BF16)

    qt_ref[0] = head_queries(0)

    def head(h, carry):
        slot = h % 2
        slabs = [slice(w0, w0 + LANES) for w0 in range(0, tt, LANES)]
        scores = []
        for ws in slabs:
            scores.append(_dot(keys_ref[2 * h], qt_ref[slot, 0:P_HALF_DIM, ws]))
            scores.append(_dot(keys_ref[2 * h + 1], qt_ref[slot, P_HALF_DIM:hq, ws]))
        qt_ref[1 - slot] = head_queries(jnp.minimum(h + 1, P_HEADS - 1))
        quick = [_top16(s, kiota) for s in scores]
        n_ranked = quick[0][2]
        for q in quick[1:]:
            n_ranked = jnp.maximum(n_ranked, q[2])
        tied = jnp.max(n_ranked) > float(P_TOPK)
        ranked = lax.cond(tied,
                          lambda: tuple(_top16_ranks(s, kiota) for s in scores),
                          lambda: tuple((q[0], q[1]) for q in quick))
        for si, ws in enumerate(slabs):
            s1, s2 = scores[2 * si], scores[2 * si + 1]
            (rank1, sv1), (rank2, sv2) = ranked[2 * si], ranked[2 * si + 1]
            cnt, z = _pair_merge(sv1, sv2, riota)
            in1 = rank1 < float(P_TOPK)
            in2 = rank2 < float(P_TOPK)
            e1 = jnp.where(in1, jnp.exp(s1 - sv1[0:1, :]) / z, 0.0)
            e2 = jnp.where(in2, jnp.exp(s2 - sv2[0:1, :]), 0.0)
            lf = jnp.zeros_like(s1)
            for r in range(P_TOPK):
                lf = jnp.where(rank1 == float(r), cnt[r:r + 1, :], lf)
            e1_ref[h, :, ws] = e1
            lf_ref[h, :, ws] = lf
            e2_ref[h, :, ws] = e2.astype(BF16)
            r2_ref[h, :, ws] = rank2.astype(BF16)
        return carry

    lax.fori_loop(0, P_HEADS, head, 0)


def _route(h1tb, wpqt, keys):
    T = h1tb.shape[1]
    tt = min(T, 512)
    osp = pl.BlockSpec((P_HEADS, P_NKEYS, tt), lambda i: (0, 0, i))
    of32 = jax.ShapeDtypeStruct((P_HEADS, P_NKEYS, T), F32)
    obf = jax.ShapeDtypeStruct((P_HEADS, P_NKEYS, T), BF16)
    return pl.pallas_call(
        functools.partial(_route_kernel, tt=tt),
        grid=(T // tt,),
        in_specs=[pl.BlockSpec((D_MODEL, tt), lambda i: (0, i)),
                  pl.BlockSpec((2 * P_HEADS * P_HALF_DIM, D_MODEL), lambda i: (0, 0)),
                  pl.BlockSpec((2 * P_HEADS, P_NKEYS, P_HALF_DIM), lambda i: (0, 0, 0))],
        out_specs=[osp, osp, osp, osp],
        out_shape=[of32, of32, obf, obf],
        scratch_shapes=[pltpu.VMEM((2, 2 * P_HALF_DIM, tt), BF16)],
        compiler_params=_cparams(("arbitrary",)),
        name="peer_route",
    )(h1tb, wpqt, keys)


PEER_EXPERT_BLOCK = 1024
PEER_SUB_BLOCK = 256


def _peer_kernel(h1t_ref, h1tb_ref, e1_ref, lf_ref, e2_ref, r2_ref, pu_ref, pvt_ref, g_ref, b_ref,
                 y_ref, acc_ref, w_ref, *, n_eb):
    eb = pl.program_id(1)
    slot = eb % 2
    rows_per_block = PEER_EXPERT_BLOCK // P_NKEYS
    sub = PEER_SUB_BLOCK
    n_sub = PEER_EXPERT_BLOCK // sub
    tt = acc_ref.shape[1]
    n_parts = 2 if tt % (2 * LANES) == 0 else 1
    pw = tt // n_parts

    def scores(sb):
        return _dot(pu_ref[sb * sub:(sb + 1) * sub, :], h1tb_ref[...])

    def out_part(p):
        return _dot(pvt_ref[...], w_ref[1 - slot, :, p * pw:(p + 1) * pw])

    def gated(a, sb):
        act = (0.5 * a * (1.0 + lax.erf(a * (2.0 ** -0.5)))).astype(BF16)
        gates = []
        for ci in range(sub // P_NKEYS):
            c = eb * rows_per_block + sb * (sub // P_NKEYS) + ci
            gate = jnp.zeros((P_NKEYS, tt), BF16)
            for h in range(P_HEADS):
                lrow = lf_ref[h, pl.ds(c, 1), :].astype(BF16)
                erow = e1_ref[h, pl.ds(c, 1), :].astype(BF16)
                picked = jnp.where(r2_ref[h] < lrow, e2_ref[h], jnp.zeros_like(gate))
                gate = gate + picked * erow
            gates.append(gate)
        w_ref[slot, sb * sub:(sb + 1) * sub, :] = jnp.concatenate(gates, axis=0) * act

    def build(with_out):
        a = [scores(0), scores(1)] + [None] * (n_sub - 2)
        outs = []
        for sb in range(n_sub):
            if with_out and sb < n_parts:
                outs.append(out_part(sb))
            gated(a[sb], sb)
            if sb + 2 < n_sub:
                a[sb + 2] = scores(sb + 2)
        for p, o in enumerate(outs):
            acc_ref[:, p * pw:(p + 1) * pw] += o

    @pl.when(eb == 0)
    def _():
        acc_ref[...] = jnp.zeros(acc_ref.shape, F32)
        build(False)

    @pl.when((eb > 0) & (eb < n_eb))
    def _():
        build(True)

    @pl.when(eb == n_eb)
    def _():
        out = jnp.concatenate([out_part(p) for p in range(n_parts)], axis=1)
        z = (ALPHA * h1t_ref[...] + acc_ref[...] + out).T
        y_ref[...] = _layer_norm_rows(z, g_ref[...], b_ref[...])


def _peer(h1t, h1tb, e1, lf, e2, r2, pu, pvt, ln_g, ln_b):
    T = h1t.shape[1]
    tt = min(T, 512)
    n_eb = P_EXPERTS // PEER_EXPERT_BLOCK
    assert PEER_EXPERT_BLOCK // PEER_SUB_BLOCK >= 2
    tsp = pl.BlockSpec((D_MODEL, tt), lambda i, e: (0, i))
    rsp = pl.BlockSpec((P_HEADS, P_NKEYS, tt), lambda i, e: (0, 0, i))
    vsp = pl.BlockSpec((1, D_MODEL), lambda i, e: (0, 0))
    return pl.pallas_call(
        functools.partial(_peer_kernel, n_eb=n_eb),
        grid=(T // tt, n_eb + 1),
        in_specs=[tsp, tsp, rsp, rsp, rsp, rsp,
                  pl.BlockSpec((PEER_EXPERT_BLOCK, D_MODEL), lambda i, e: (jnp.minimum(e, n_eb - 1), 0)),
                  pl.BlockSpec((D_MODEL, PEER_EXPERT_BLOCK), lambda i, e: (0, jnp.maximum(e - 1, 0))),
                  vsp, vsp],
        out_specs=pl.BlockSpec((tt, D_MODEL), lambda i, e: (i, 0)),
        out_shape=jax.ShapeDtypeStruct((T, D_MODEL), F32),
        scratch_shapes=[pltpu.VMEM((D_MODEL, tt), F32), pltpu.VMEM((2, PEER_EXPERT_BLOCK, tt), BF16)],
        compiler_params=_cparams(("arbitrary", "arbitrary")),
        name="peer_dense",
    )(h1t, h1tb, e1, lf, e2, r2, pu, pvt, ln_g, ln_b)


def _gate_layouts(g, B, S_real, S_pad):
    H, L = M_HEADS, M_CHUNK
    gi = g[:, :H].reshape(B, S_real, H)
    gf = g[:, H:2 * H].reshape(B, S_real, H)
    if S_pad != S_real:
        pw = ((0, 0), (0, S_pad - S_real), (0, 0))
        gi = jnp.pad(gi, pw, constant_values=PAD_LOG_INPUT_GATE)
        gf = jnp.pad(gf, pw, constant_values=-PAD_LOG_INPUT_GATE)
    gi = jnp.swapaxes(gi, 1, 2)
    gf = jnp.swapaxes(gf, 1, 2)
    col = lambda a: a.reshape(B, H, S_pad, 1)
    row = lambda a: a.reshape(B, H, S_pad // L, L)
    return col(gi), col(gf), row(gi), row(gf)


def _pad_seq(a, B, S_real, S_pad):
    if S_pad == S_real:
        return a
    a = a.reshape(B, S_real, a.shape[-1])
    return jnp.pad(a, ((0, 0), (0, S_pad - S_real), (0, 0))).reshape(B * S_pad, a.shape[-1])


def _group(x3, attend, state, W):
    B, S, _ = x3.shape
    T = B * S
    x = x3.reshape(T, D_MODEL)
    q, k, kb, v, u, vm, om, ga, gb, g, vbt = _in_proj(x, W["w_main"], W["b_main"], W["w_gate"],
                                                      W["b_gate"])
    oa = attend(q, k, v, kb, vbt)

    c0, n0, m0, conv0 = state
    S_pad = -(-S // M_CHUNK) * M_CHUNK
    icol, fcol, irow, frow = _gate_layouts(g, B, S, S_pad)
    halo = jnp.pad(conv0, ((0, 0), (CONV_HALO - (M_CONV - 1), 0), (0, 0)))
    hb, c_new, n_new, m_new = _mlstm(
        _pad_seq(u, B, S, S_pad), halo, _pad_seq(vm, B, S, S_pad), _pad_seq(om, B, S, S_pad),
        icol, fcol, irow, frow, W["w_conv"], W["b_conv"], W["w_qm"], W["w_km"], W["w_kmt"], W["mnorm_g"],
        c0, n0.reshape(B, M_HEADS, 1, M_HEAD_DIM),
        jnp.broadcast_to(m0[:, :, None, None], (B, M_HEADS, 1, LANES)), B, S_pad)
    if S_pad != S:
        hb = hb.reshape(B, S_pad, D_MODEL)[:, :S].reshape(T, D_MODEL)
    u3 = u.reshape(B, S, D_MODEL)
    conv_new = jnp.concatenate([conv0, u3], axis=1)[:, S:]

    h1t, h1tb = _merge(x, oa, hb, ga, gb, W["w_a"], W["w_b"], W["w_o"], W["ln1_g"], W["ln1_b"])
    e1, lf, e2, r2 = _route(h1tb, W["w_pqt"], W["p_keys"])
    y = _peer(h1t, h1tb, e1, lf, e2, r2, W["p_u"], W["p_vt"], W["ln2_g"], W["ln2_b"])

    new_state = (k.reshape(1, B, S, A_HEADS, A_HD2), v.reshape(1, B, S, A_HEADS, A_HD2),
                 c_new[None], n_new.reshape(1, B, M_HEADS, M_HEAD_DIM), m_new[None, :, :, 0, 0],
                 conv_new[None])
    return y.reshape(B, S, D_MODEL), new_state


def kernel(x_prompt, x_sample, cache_k, cache_v, state_C, state_n, state_m, state_conv, page_table,
           w_in, b_in, lam_q1, lam_k1, lam_q2, lam_k2, subln_g, w_conv, b_conv, w_qm, w_km, mnorm_g,
           w_a, w_b, w_o, ln1_g, ln1_b, w_pq, p_keys, p_u, p_v, ln2_g, ln2_b):
    assert w_in.shape[0] == DEPTH
    B, S, _ = x_prompt.shape
    DB, DS, _ = x_sample.shape
    assert 2 * DS == SAMPLE_ROWS

    n_pre = 6 * D_MODEL
    wi, bi = w_in[0], b_in[0]
    W = {
        "w_main": jnp.concatenate([wi[:, :n_pre], wi[:, n_pre + GATE_COLS:]], axis=1).astype(BF16),
        "b_main": jnp.concatenate([bi[:n_pre], bi[n_pre + GATE_COLS:]])[None],
        "w_gate": jnp.pad(wi[:, n_pre:n_pre + GATE_COLS], ((0, 0), (0, LANES - GATE_COLS))).astype(BF16),
        "b_gate": jnp.pad(bi[n_pre:n_pre + GATE_COLS], (0, LANES - GATE_COLS))[None],
        "w_conv": w_conv[0], "b_conv": b_conv[0][None],
        "w_qm": w_qm[0].astype(BF16), "w_km": w_km[0].astype(BF16),
        "w_kmt": jnp.swapaxes(w_km[0], 1, 2).astype(BF16),
        "mnorm_g": mnorm_g[0][None],
        "w_a": w_a[0].astype(BF16), "w_b": w_b[0].astype(BF16), "w_o": w_o[0].astype(BF16),
        "ln1_g": ln1_g[0][None], "ln1_b": ln1_b[0][None],
        "w_pqt": w_pq[0].T.astype(BF16),
        "p_keys": p_keys[0].reshape(2 * P_HEADS, P_NKEYS, P_HALF_DIM).astype(BF16),
        "p_u": p_u[0].astype(BF16), "p_vt": p_v[0].T.astype(BF16),
        "ln2_g": ln2_g[0][None], "ln2_b": ln2_b[0][None],
    }
    lam_vecs = (lam_q1[0][None], lam_k1[0][None], lam_q2[0][None], lam_k2[0][None])
    sg = subln_g[0][None]

    def attend_prompt(q, k, v, kb, vbt):
        return _attn_prompt(q, kb, vbt, lam_vecs, subln_g[0][:, None], B, S)

    def attend_sample(q, k, v, kb, vbt):
        q3 = q.reshape(DB, DS, D_MODEL)
        q8 = jnp.concatenate([q3, q3], axis=1)
        padn = ((0, 0), (0, SAMPLE_ROWS - DS), (0, 0), (0, 0))
        kn8 = jnp.pad(k.reshape(DB, DS, A_HEADS, A_HD2), padn)
        vn8 = jnp.pad(v.reshape(DB, DS, A_HEADS, A_HD2), padn)
        o8 = _attn_sample(q8, kn8, vn8, cache_k, cache_v, page_table, lam_vecs, sg)
        return o8[:, :DS].reshape(DB * DS, D_MODEL).astype(BF16)

    zero_state = (jnp.zeros((B, M_HEADS, M_HEAD_DIM, M_HEAD_DIM), F32),
                  jnp.zeros((B, M_HEADS, M_HEAD_DIM), F32),
                  jnp.zeros((B, M_HEADS), F32),
                  jnp.zeros((B, M_CONV - 1, D_MODEL), F32))
    yp, sp = _group(x_prompt, attend_prompt, zero_state, W)
    ys, ss = _group(x_sample, attend_sample,
                    (state_C[0], state_n[0], state_m[0], state_conv[0]), W)
    return (yp, ys) + sp + ss
```

```python
import functools
import math

import jax
import jax.numpy as jnp
from jax import lax
from jax.experimental import pallas as pl
from jax.experimental.pallas import tpu as pltpu

F32 = jnp.float32
BF16 = jnp.bfloat16

D_MODEL = 1024
A_HEADS = 8
A_HEAD_DIM = 64
A_HD2 = 2 * A_HEAD_DIM
M_HEADS = 4
M_HEAD_DIM = 256
M_CONV = 4
M_CHUNK = 64
P_HEADS = 8
P_NKEYS = 128
P_HALF_DIM = 128
P_TOPK = 16
P_EXPERTS = P_NKEYS * P_NKEYS
PAGE_SIZE = 128
DEPTH = 1
ALPHA = (2.0 * DEPTH) ** 0.25
LN_EPS = 1e-5
PAD_LOG_INPUT_GATE = -1e30
LAM_INIT = 0.8 - 0.6 * math.exp(-0.3 * 0)

N_MAIN_GROUPS = 8
GATE_COLS = 2 * M_HEADS
LANES = 128
VMEM_LIMIT = 56 * 1024 * 1024

NT_DIMS = (((1,), (1,)), ((), ()))


def _cparams(sem):
    return pltpu.CompilerParams(dimension_semantics=sem, vmem_limit_bytes=VMEM_LIMIT)


def _dot(a, b):
    return jnp.dot(a, b, preferred_element_type=F32)


def _dot_nt(a, b):
    return lax.dot_general(a, b, NT_DIMS, preferred_element_type=F32)


def _in_proj_kernel(x_ref, w_ref, b_ref, wg_ref, bg_ref,
                    q_ref, k_ref, kb_ref, v_ref, u_ref, vm_ref, om_ref, ga_ref, gb_ref,
                    g_ref, vbt_ref):
    xb = x_ref[...].astype(BF16)

    def proj(j):
        cols = slice(j * D_MODEL, (j + 1) * D_MODEL)
        return _dot(xb, w_ref[:, cols]) + b_ref[:, cols]

    q_ref[...] = (proj(0) * (A_HEAD_DIM ** -0.5)).astype(BF16)
    g_ref[...] = _dot(xb, wg_ref[...]) + bg_ref[...]
    k_ref[...] = proj(1)
    kb_ref[...] = k_ref[...].astype(BF16)
    v_ref[...] = proj(2)
    vbt_ref[...] = v_ref[...].T.astype(BF16)
    u_ref[...] = proj(3)
    vm_ref[...] = proj(4).astype(BF16)
    om_ref[...] = proj(5)
    ga_ref[...] = proj(6)
    gb_ref[...] = proj(7)


IN_PROJ_ROWS = 256


def _in_proj(x, w_main, b_main, w_gate, b_gate):
    T = x.shape[0]
    tm = min(T, IN_PROJ_ROWS)
    assert T % tm == 0
    row = lambda i: (i, 0)
    whole = lambda i: (0, 0)
    f32o = jax.ShapeDtypeStruct((T, D_MODEL), F32)
    bf16o = jax.ShapeDtypeStruct((T, D_MODEL), BF16)
    ncol = N_MAIN_GROUPS * D_MODEL
    return pl.pallas_call(
        _in_proj_kernel,
        grid=(T // tm,),
        in_specs=[
            pl.BlockSpec((tm, D_MODEL), row),
            pl.BlockSpec((D_MODEL, ncol), whole),
            pl.BlockSpec((1, ncol), whole),
            pl.BlockSpec((D_MODEL, LANES), whole),
            pl.BlockSpec((1, LANES), whole),
        ],
        out_specs=[pl.BlockSpec((tm, D_MODEL), row)] * 9 + [pl.BlockSpec((tm, LANES), row),
                                                           pl.BlockSpec((D_MODEL, tm), lambda i: (0, i))],
        out_shape=[bf16o, f32o, bf16o, f32o, f32o, bf16o, f32o, f32o, f32o,
                   jax.ShapeDtypeStruct((T, LANES), F32), jax.ShapeDtypeStruct((D_MODEL, T), BF16)],
        compiler_params=_cparams(("arbitrary",)),
        name="in_proj",
    )(x, w_main, b_main, w_gate, b_gate)


def _lam_value(lq1, lk1, lq2, lk2):
    a = jnp.exp(jnp.sum(lq1 * lk1, axis=1, keepdims=True))
    b = jnp.exp(jnp.sum(lq2 * lk2, axis=1, keepdims=True))
    return a - b + LAM_INIT


def _sub_norm(o, g):
    r = lax.rsqrt(jnp.mean(o * o, axis=1, keepdims=True) + LN_EPS)
    return o * r * g * (1.0 - LAM_INIT)


ATTN_TQ = 256


def _attn_prompt_kernel(lq1_ref, lk1_ref, lq2_ref, lk2_ref, gcol_ref, q_ref, k_ref, vt_ref, o_ref, *, nq):
    i = pl.program_id(2)
    tq = ATTN_TQ
    lam = _lam_value(lq1_ref[...], lk1_ref[...], lq2_ref[...], lk2_ref[...])
    krow = lax.broadcasted_iota(jnp.int32, (tq, tq), 0)
    qcol = lax.broadcasted_iota(jnp.int32, (tq, tq), 1)
    hcols = [slice(hh * A_HD2, (hh + 1) * A_HD2) for hh in range(ATTN_HEADS_PER_STEP)]

    def query_maps(hh):
        q = q_ref[:, hcols[hh]]
        lane = lax.broadcasted_iota(jnp.int32, q.shape, 1)
        zero = jnp.zeros_like(q)
        return jnp.where(lane < A_HEAD_DIM, q, zero), jnp.where(lane >= A_HEAD_DIM, q, zero)

    def scores(qm, n, hh):
        lo, hi = n * tq, (n + 1) * tq
        sd = _dot_nt(k_ref[lo:hi, hcols[hh]], qm)
        st = _dot_nt(k_ref[0:lo, hcols[hh]], qm) if n else None
        return sd, st

    def attend(sd, st, n, hh):
        lo, hi = n * tq, (n + 1) * tq
        sd = jnp.where(krow <= qcol, sd, -jnp.inf)
        m = jnp.max(sd, axis=0, keepdims=True)
        if n:
            m = jnp.maximum(m, jnp.max(st, axis=0, keepdims=True))
        pd = jnp.exp(sd - m)
        l = jnp.sum(pd, axis=0, keepdims=True)
        acc = _dot(vt_ref[hcols[hh], lo:hi], pd.astype(BF16))
        if n:
            pt = jnp.exp(st - m)
            l = l + jnp.sum(pt, axis=0, keepdims=True)
            acc = acc + _dot(vt_ref[hcols[hh], 0:lo], pt.astype(BF16))
        return acc / l

    for n in range(nq):
        @pl.when(i == n)
        def _(n=n):
            maps = [query_maps(hh) for hh in range(ATTN_HEADS_PER_STEP)]
            sc = [(scores(m1, n, hh), scores(m2, n, hh)) for hh, (m1, m2) in enumerate(maps)]
            for hh, (s1, s2) in enumerate(sc):
                o = attend(*s1, n, hh) - lam * attend(*s2, n, hh)
                r = lax.rsqrt(jnp.mean(o * o, axis=0, keepdims=True) + LN_EPS)
                o = o * r * gcol_ref[...] * (1.0 - LAM_INIT)
                o_ref[:, hcols[hh]] = o.T.astype(BF16)


ATTN_HEADS_PER_STEP = 4


def _attn_prompt(q, kb, vbt, lam_vecs, subln_gcol, B, S):
    tq = ATTN_TQ
    assert S % tq == 0 and A_HEADS % ATTN_HEADS_PER_STEP == 0
    nq = S // tq
    hw = ATTN_HEADS_PER_STEP * A_HD2
    vec = pl.BlockSpec((1, A_HEAD_DIM), lambda b, h, i: (0, 0))
    return pl.pallas_call(
        functools.partial(_attn_prompt_kernel, nq=nq),
        grid=(B, A_HEADS // ATTN_HEADS_PER_STEP, nq),
        in_specs=[vec, vec, vec, vec,
                  pl.BlockSpec((A_HD2, 1), lambda b, h, i: (0, 0)),
                  pl.BlockSpec((tq, hw), lambda b, h, i: (b * nq + i, h)),
                  pl.BlockSpec((S, hw), lambda b, h, i: (b, h)),
                  pl.BlockSpec((hw, S), lambda b, h, i: (h, b))],
        out_specs=pl.BlockSpec((tq, hw), lambda b, h, i: (b * nq + i, h)),
        out_shape=jax.ShapeDtypeStruct((B * S, D_MODEL), BF16),
        compiler_params=_cparams(("arbitrary", "arbitrary", "arbitrary")),
        name="attn_prompt",
    )(*lam_vecs, subln_gcol, q, kb, vbt)


SAMPLE_ROWS = 8
PAGES_PER_STEP = 16


def _attn_sample_kernel(pt_ref, lq1_ref, lk1_ref, lq2_ref, lk2_ref, g_ref, q_ref, kn_ref, vn_ref, *rest,
                        n_steps, dec_seq):
    pp = PAGES_PER_STEP
    k_refs, v_refs = rest[:pp], rest[pp:2 * pp]
    o_ref, qf_ref, bias_ref, m_ref, l_ref, acc_ref = rest[2 * pp:]
    step_id = pl.program_id(1)
    rows = A_HEADS * SAMPLE_ROWS
    flat = PAGE_SIZE * A_HEADS

    @pl.when(step_id == 0)
    def _():
        q8 = q_ref[0]
        qf = jnp.concatenate([q8[:, h * A_HD2:(h + 1) * A_HD2] for h in range(A_HEADS)], axis=0)
        row = lax.broadcasted_iota(jnp.int32, qf.shape, 0)
        lane = lax.broadcasted_iota(jnp.int32, qf.shape, 1)
        first_map = (row % SAMPLE_ROWS) < dec_seq
        qf_ref[...] = jnp.where(first_map == (lane < A_HEAD_DIM), qf, jnp.zeros_like(qf))
        brow = lax.broadcasted_iota(jnp.int32, (rows, flat), 0)
        bcol = lax.broadcasted_iota(jnp.int32, (rows, flat), 1)
        bias_ref[...] = jnp.where((bcol % A_HEADS) == (brow // SAMPLE_ROWS), 0.0, -jnp.inf)
        m_ref[...] = jnp.full(m_ref.shape, -jnp.inf, F32)
        l_ref[...] = jnp.zeros(l_ref.shape, F32)
        acc_ref[...] = jnp.zeros(acc_ref.shape, F32)

    def update(kfs, vfs, bias):
        ss = [_dot_nt(qf_ref[...], kf) + bias for kf in kfs]
        m = m_ref[...]
        m_new = m
        for s in ss:
            m_new = jnp.maximum(m_new, jnp.max(s, axis=1, keepdims=True))
        alpha = jnp.exp(m - m_new)
        l = alpha * l_ref[...]
        acc = alpha * acc_ref[...]
        for s, vf in zip(ss, vfs):
            p = jnp.exp(s - m_new)
            l = l + jnp.sum(p, axis=1, keepdims=True)
            acc = acc + _dot(p.astype(BF16), vf)
        l_ref[...] = l
        acc_ref[...] = acc
        m_ref[...] = m_new

    update([k_refs[t][0, 0].reshape(flat, A_HD2).astype(BF16) for t in range(pp)],
           [v_refs[t][0, 0].reshape(flat, A_HD2).astype(BF16) for t in range(pp)], bias_ref[...])

    @pl.when(step_id == n_steps - 1)
    def _():
        nnew = SAMPLE_ROWS * A_HEADS
        pad = jnp.zeros((PAGE_SIZE - nnew, A_HD2), BF16)
        kf = jnp.concatenate([kn_ref[0].reshape(nnew, A_HD2).astype(BF16), pad], axis=0)
        vf = jnp.concatenate([vn_ref[0].reshape(nnew, A_HD2).astype(BF16), pad], axis=0)
        row = lax.broadcasted_iota(jnp.int32, (rows, PAGE_SIZE), 0)
        col = lax.broadcasted_iota(jnp.int32, (rows, PAGE_SIZE), 1)
        ok = ((col % A_HEADS) == (row // SAMPLE_ROWS)) & ((col // A_HEADS) <= (row % dec_seq))
        update([kf], [vf], jnp.where(ok, 0.0, -jnp.inf))
        lam = _lam_value(lq1_ref[...], lk1_ref[...], lq2_ref[...], lk2_ref[...])
        g = g_ref[...]
        for h in range(A_HEADS):
            rs = slice(h * SAMPLE_ROWS, (h + 1) * SAMPLE_ROWS)
            o = acc_ref[rs, :] / l_ref[rs, :]
            o = o - lam * pltpu.roll(o, SAMPLE_ROWS - dec_seq, 0)
            o_ref[0, :, h * A_HD2:(h + 1) * A_HD2] = _sub_norm(o, g)


def _attn_sample(q8, kn8, vn8, cache_k, cache_v, page_table, lam_vecs, subln_g):
    DB, n_pages = page_table.shape
    pp = PAGES_PER_STEP
    assert n_pages % pp == 0
    n_steps = n_pages // pp
    dec_seq = SAMPLE_ROWS // 2
    rows = A_HEADS * SAMPLE_ROWS
    vec = pl.BlockSpec((1, A_HEAD_DIM), lambda b, s, pt: (0, 0))
    per_b = pl.BlockSpec((1, SAMPLE_ROWS, D_MODEL), lambda b, s, pt: (b, 0, 0))
    new_kv = pl.BlockSpec((1, SAMPLE_ROWS, A_HEADS, A_HD2), lambda b, s, pt: (b, 0, 0, 0))

    def page_spec(t):
        return pl.BlockSpec((1, 1, PAGE_SIZE, A_HEADS, A_HD2),
                            lambda b, s, pt: (0, pt[b, s * pp + t], 0, 0, 0))

    grid_spec = pltpu.PrefetchScalarGridSpec(
        num_scalar_prefetch=1,
        grid=(DB, n_steps),
        in_specs=[vec, vec, vec, vec, pl.BlockSpec((1, A_HD2), lambda b, s, pt: (0, 0)),
                  per_b, new_kv, new_kv]
                 + [page_spec(t) for t in range(pp)] + [page_spec(t) for t in range(pp)],
        out_specs=per_b,
        scratch_shapes=[pltpu.VMEM((rows, A_HD2), BF16),
                        pltpu.VMEM((rows, PAGE_SIZE * A_HEADS), F32),
                        pltpu.VMEM((rows, 1), F32), pltpu.VMEM((rows, 1), F32),
                        pltpu.VMEM((rows, A_HD2), F32)],
    )
    return pl.pallas_call(
        functools.partial(_attn_sample_kernel, n_steps=n_steps, dec_seq=dec_seq),
        grid_spec=grid_spec,
        out_shape=jax.ShapeDtypeStruct((DB, SAMPLE_ROWS, D_MODEL), F32),
        compiler_params=_cparams(("arbitrary", "arbitrary")),
        name="attn_sample",
    )(page_table, *lam_vecs, subln_g, q8, kn8, vn8, *([cache_k] * pp), *([cache_v] * pp))


CONV_HALO = 8
MLSTM_HEADS_PER_STEP = 2


def _log_sigmoid(x):
    return -(jnp.maximum(-x, 0.0) + jnp.log1p(jnp.exp(-jnp.abs(x))))


def _mlstm_kernel(u_ref, halo_ref, vm_ref, om_ref, icol_ref, fcol_ref, irow_ref, frow_ref,
                  wconv_ref, bconv_ref, wq_ref, wk_ref, wkt_ref, g_ref, c0_ref, n0_ref, m0_ref,
                  hb_ref, c_ref, n_ref, m_ref,
                  ext_ref, uc_ref, q_ref, k_ref, ct_ref, nrow_ref, *, S, L):
    kscale = M_HEAD_DIM ** -0.5
    dh = M_HEAD_DIM
    heads = range(MLSTM_HEADS_PER_STEP)
    hcols = [slice(hh * dh, (hh + 1) * dh) for hh in heads]
    ext_ref[0:CONV_HALO, :] = halo_ref[0]
    ext_ref[CONV_HALO:CONV_HALO + S, :] = u_ref[...]
    rb = min(S, 512)
    for r0 in range(0, S, rb):
        acc = jnp.broadcast_to(bconv_ref[...], (rb, ext_ref.shape[1]))
        for jj in range(M_CONV):
            off = CONV_HALO - (M_CONV - 1) + jj + r0
            acc = acc + ext_ref[off:off + rb, :] * wconv_ref[jj:jj + 1, :]
        uc = (acc * jax.nn.sigmoid(acc)).astype(BF16)
        uc_ref[r0:r0 + rb, :] = uc
        for hh in heads:
            q_ref[r0:r0 + rb, hcols[hh]] = _dot(uc[:, hcols[hh]], wq_ref[hh]).astype(BF16)
            k_ref[r0:r0 + rb, hcols[hh]] = (_dot(uc[:, hcols[hh]], wk_ref[hh]) * kscale).astype(BF16)

    for hh in heads:
        ct_ref[hh] = c0_ref[0, hh].T
        nrow_ref[hh] = n0_ref[0, hh]
    ti = lax.broadcasted_iota(jnp.int32, (L, L), 0)
    si = lax.broadcasted_iota(jnp.int32, (L, L), 1)
    tril = si <= ti
    gain = g_ref[...]

    def chunk(c, m_prevs):
        r0 = pl.multiple_of(c * L, L)
        rows = pl.ds(r0, L)
        qcs = [q_ref[rows, hcols[hh]] for hh in heads]
        vcs = [vm_ref[rows, hcols[hh]] for hh in heads]
        kcts = [(_dot_nt(wkt_ref[hh], uc_ref[rows, hcols[hh]]) * kscale).astype(BF16) for hh in heads]
        sqk = [_dot(qcs[hh], kcts[hh]) for hh in heads]
        qct = [_dot(qcs[hh], ct_ref[hh].astype(BF16)) for hh in heads]
        gm = []
        for hh in heads:
            m_prev = m_prevs[hh]
            icol = icol_ref[0, hh, rows, :]
            fcol = _log_sigmoid(fcol_ref[0, hh, rows, :])
            irow = irow_ref[0, hh, pl.ds(c, 1), :]
            frow = _log_sigmoid(frow_ref[0, hh, pl.ds(c, 1), :])
            b_col = jnp.sum(jnp.where(tril, frow, 0.0), axis=1, keepdims=True)
            b_row = jnp.sum(jnp.where(ti <= si, fcol, 0.0), axis=0, keepdims=True)
            dmat = jnp.where(tril, b_col - b_row + irow, -jnp.inf)
            m_t = jnp.maximum(b_col + m_prev, jnp.max(dmat, axis=1, keepdims=True))
            wmat = jnp.exp(dmat - m_t)
            inter = jnp.exp(b_col + m_prev - m_t)
            m_new = m_t[L - 1:L, :]
            b_last = b_col[L - 1:L, :]
            decay = jnp.exp(b_last + m_prev - m_new)
            w_col = jnp.exp(b_last - b_col + icol - m_new)
            gm.append((m_t, wmat, inter, m_new, decay, w_col))
        sws = [sqk[hh] * gm[hh][1] for hh in heads]
        wvs = [(gm[hh][5] * vcs[hh].astype(F32)).astype(BF16) for hh in heads]
        swv = [_dot(sws[hh].astype(BF16), vcs[hh]) for hh in heads]
        upd = [_dot(kcts[hh], wvs[hh]) for hh in heads]
        m_news = []
        for hh in heads:
            m_t, wmat, inter, m_new, decay, w_col = gm[hh]
            qc = qcs[hh]
            kc = k_ref[rows, hcols[hh]]
            num = inter * qct[hh] + swv[hh]
            qn = jnp.sum(qc.astype(F32) * nrow_ref[hh], axis=1, keepdims=True)
            dot = inter * qn + jnp.sum(sws[hh], axis=1, keepdims=True)
            h = num / jnp.maximum(jnp.abs(dot), jnp.exp(-m_t))
            ct_ref[hh] = decay * ct_ref[hh] + upd[hh]
            nrow_ref[hh] = decay * nrow_ref[hh] + jnp.sum(w_col * kc.astype(F32), axis=0, keepdims=True)
            mu = jnp.mean(h, axis=1, keepdims=True)
            hc = h - mu
            var = jnp.mean(hc * hc, axis=1, keepdims=True)
            hn = hc * lax.rsqrt(var + LN_EPS) * gain[:, hcols[hh]]
            hb_ref[rows, hcols[hh]] = (jax.nn.sigmoid(om_ref[rows, hcols[hh]]) * hn).astype(BF16)
            m_news.append(m_new)
        return tuple(m_news)

    m_fin = lax.fori_loop(0, S // L, chunk, tuple(m0_ref[0, hh][:, 0:1] for hh in heads))
    for hh in heads:
        c_ref[0, hh] = ct_ref[hh].T
        n_ref[0, hh] = nrow_ref[hh]
        m_ref[0, hh] = jnp.broadcast_to(m_fin[hh], (1, LANES))


def _mlstm(u, halo, vm, om, icol, fcol, irow, frow, w_conv, b_conv, wq, wk, wkt, mnorm_g,
           c0, n0, m0, B, S):
    L = M_CHUNK
    nc = S // L
    H, dh = M_HEADS, M_HEAD_DIM
    hp = MLSTM_HEADS_PER_STEP
    assert H % hp == 0
    seq = pl.BlockSpec((S, hp * dh), lambda b, h: (b, h))
    col = pl.BlockSpec((1, hp, S, 1), lambda b, h: (b, h, 0, 0))
    rowg = pl.BlockSpec((1, hp, nc, L), lambda b, h: (b, h, 0, 0))
    wmat = pl.BlockSpec((hp, dh, dh), lambda b, h: (h, 0, 0))
    state = pl.BlockSpec((1, hp, dh, dh), lambda b, h: (b, h, 0, 0))
    nvec = pl.BlockSpec((1, hp, 1, dh), lambda b, h: (b, h, 0, 0))
    mvec = pl.BlockSpec((1, hp, 1, LANES), lambda b, h: (b, h, 0, 0))
    return pl.pallas_call(
        functools.partial(_mlstm_kernel, S=S, L=L),
        grid=(B, H // hp),
        in_specs=[seq,
                  pl.BlockSpec((1, CONV_HALO, hp * dh), lambda b, h: (b, 0, h)),
                  seq, seq, col, col, rowg, rowg,
                  pl.BlockSpec((M_CONV, hp * dh), lambda b, h: (0, h)),
                  pl.BlockSpec((1, hp * dh), lambda b, h: (0, h)),
                  wmat, wmat, wmat,
                  pl.BlockSpec((1, hp * dh), lambda b, h: (0, h)),
                  state, nvec, mvec],
        out_specs=[seq, state, nvec, mvec],
        out_shape=[jax.ShapeDtypeStruct((B * S, D_MODEL), BF16),
                   jax.ShapeDtypeStruct((B, H, dh, dh), F32),
                   jax.ShapeDtypeStruct((B, H, 1, dh), F32),
                   jax.ShapeDtypeStruct((B, H, 1, LANES), F32)],
        scratch_shapes=[pltpu.VMEM((S + CONV_HALO, hp * dh), F32),
                        pltpu.VMEM((S, hp * dh), BF16), pltpu.VMEM((S, hp * dh), BF16),
                        pltpu.VMEM((S, hp * dh), BF16),
                        pltpu.VMEM((hp, dh, dh), F32), pltpu.VMEM((hp, 1, dh), F32)],
        compiler_params=_cparams(("arbitrary", "arbitrary")),
        name="mlstm",
    )(u, halo, vm, om, icol, fcol, irow, frow, w_conv, b_conv, wq, wk, wkt, mnorm_g, c0, n0, m0)


def _layer_norm_rows(z, g, b):
    mu = jnp.mean(z, axis=1, keepdims=True)
    zc = z - mu
    var = jnp.mean(zc * zc, axis=1, keepdims=True)
    return zc * lax.rsqrt(var + LN_EPS) * g + b


def _merge_kernel(x_ref, oa_ref, hb_ref, ga_ref, gb_ref, wa_ref, wb_ref, wo_ref, g_ref, b_ref,
                  h1t_ref, h1tb_ref):
    ya = _dot(oa_ref[...], wa_ref[...])
    yb = _dot(hb_ref[...], wb_ref[...])
    merged = jax.nn.sigmoid(ga_ref[...]) * ya + jax.nn.sigmoid(gb_ref[...]) * yb
    out = _dot(merged.astype(BF16), wo_ref[...])
    h1 = _layer_norm_rows(ALPHA * x_ref[...] + out, g_ref[...], b_ref[...])
    h1t = h1.T
    h1t_ref[...] = h1t
    h1tb_ref[...] = h1t.astype(BF16)


def _merge(x, oa, hb, ga, gb, wa, wb, wo, ln_g, ln_b):
    T = x.shape[0]
    tm = min(T, 512)
    tok = pl.BlockSpec((tm, D_MODEL), lambda i: (i, 0))
    wsp = pl.BlockSpec((D_MODEL, D_MODEL), lambda i: (0, 0))
    vsp = pl.BlockSpec((1, D_MODEL), lambda i: (0, 0))
    tsp = pl.BlockSpec((D_MODEL, tm), lambda i: (0, i))
    return pl.pallas_call(
        _merge_kernel,
        grid=(T // tm,),
        in_specs=[tok, tok, tok, tok, tok, wsp, wsp, wsp, vsp, vsp],
        out_specs=[tsp, tsp],
        out_shape=[jax.ShapeDtypeStruct((D_MODEL, T), F32), jax.ShapeDtypeStruct((D_MODEL, T), BF16)],
        compiler_params=_cparams(("arbitrary",)),
        name="merge_ln1",
    )(x, oa, hb, ga, gb, wa, wb, wo, ln_g, ln_b)


NOT_SELECTED = 255.0


def _top16_ranks(s, iota):
    rank = jnp.full(s.shape, NOT_SELECTED, F32)
    tops = []
    for r in range(P_TOPK):
        mx = jnp.max(s, axis=0, keepdims=True)
        idx = jnp.min(jnp.where(s == mx, iota, float(P_NKEYS)), axis=0, keepdims=True)
        sel = iota == idx
        rank = jnp.where(sel, float(r), rank)
        s = jnp.where(sel, -jnp.inf, s)
        tops.append(mx)
    return rank, jnp.concatenate(tops, axis=0)


def _top16(s, iota):
    rank = jnp.full(s.shape, NOT_SELECTED, F32)
    tops = []
    for r in range(P_TOPK):
        mx = jnp.max(s, axis=0, keepdims=True)
        eq = s == mx
        rank = jnp.where(eq, float(r), rank)
        s = jnp.where(eq, -jnp.inf, s)
        tops.append(mx)
    n_ranked = jnp.sum(jnp.where(rank < float(P_TOPK), 1.0, 0.0), axis=0, keepdims=True)
    return rank, jnp.concatenate(tops, axis=0), n_ranked


def _pair_merge(sv1, sv2, riota):
    cnt = jnp.zeros(sv1.shape, F32)
    front = sv1 + sv2[0:1, :]
    top0 = front[0:1, :]
    z = jnp.zeros_like(top0)
    for _ in range(P_TOPK):
        fm = jnp.max(front, axis=0, keepdims=True)
        idx = jnp.min(jnp.where(front == fm, riota, float(P_TOPK)), axis=0, keepdims=True)
        sel = riota == idx
        z = z + jnp.exp(fm - top0)
        cnt = cnt + jnp.where(sel, 1.0, 0.0)
        csel = jnp.sum(jnp.where(sel, cnt, 0.0), axis=0, keepdims=True)
        v1 = jnp.sum(jnp.where(sel, sv1, 0.0), axis=0, keepdims=True)
        v2 = jnp.sum(jnp.where(riota == csel, sv2, 0.0), axis=0, keepdims=True)
        nxt = jnp.where(csel < float(P_TOPK), v1 + v2, -jnp.inf)
        front = jnp.where(sel, nxt, front)
    return cnt, z


def _route_kernel(h1t_ref, wpqt_ref, keys_ref, e1_ref, lf_ref, e2_ref, r2_ref, qt_ref, *, tt):
    kiota = lax.broadcasted_iota(jnp.int32, (P_NKEYS, LANES), 0).astype(F32)
    riota = lax.broadcasted_iota(jnp.int32, (P_TOPK, LANES), 0).astype(F32)
    hq = 2 * P_HALF_DIM

    def head_queries(h):
        rows = pl.ds(pl.multiple_of(h * hq, hq), hq)
        return _dot(wpqt_ref[rows, :], h1t_ref[...]).astype(BF16)

    qt_ref[0] = head_queries(0)

    def head(h, carry):
        slot = h % 2
        slabs = [slice(w0, w0 + LANES) for w0 in range(0, tt, LANES)]
        scores = []
        for ws in slabs:
            scores.append(_dot(keys_ref[2 * h], qt_ref[slot, 0:P_HALF_DIM, ws]))
            scores.append(_dot(keys_ref[2 * h + 1], qt_ref[slot, P_HALF_DIM:hq, ws]))
        qt_ref[1 - slot] = head_queries(jnp.minimum(h + 1, P_HEADS - 1))
        quick = [_top16(s, kiota) for s in scores]
        n_ranked = quick[0][2]
        for q in quick[1:]:
            n_ranked = jnp.maximum(n_ranked, q[2])
        tied = jnp.max(n_ranked) > float(P_TOPK)
        ranked = lax.cond(tied,
                          lambda: tuple(_top16_ranks(s, kiota) for s in scores),
                          lambda: tuple((q[0], q[1]) for q in quick))
        for si, ws in enumerate(slabs):
            s1, s2 = scores[2 * si], scores[2 * si + 1]
            (rank1, sv1), (rank2, sv2) = ranked[2 * si], ranked[2 * si + 1]
            cnt, z = _pair_merge(sv1, sv2, riota)
            in1 = rank1 < float(P_TOPK)
            in2 = rank2 < float(P_TOPK)
            e1 = jnp.where(in1, jnp.exp(s1 - sv1[0:1, :]) / z, 0.0)
            e2 = jnp.where(in2, jnp.exp(s2 - sv2[0:1, :]), 0.0)
            lf = jnp.zeros_like(s1)
            for r in range(P_TOPK):
                lf = jnp.where(rank1 == float(r), cnt[r:r + 1, :], lf)
            e1_ref[h, :, ws] = e1
            lf_ref[h, :, ws] = lf
            e2_ref[h, :, ws] = e2.astype(BF16)
            r2_ref[h, :, ws] = rank2.astype(BF16)
        return carry

    lax.fori_loop(0, P_HEADS, head, 0)


def _route(h1tb, wpqt, keys):
    T = h1tb.shape[1]
    tt = min(T, 512)
    osp = pl.BlockSpec((P_HEADS, P_NKEYS, tt), lambda i: (0, 0, i))
    of32 = jax.ShapeDtypeStruct((P_HEADS, P_NKEYS, T), F32)
    obf = jax.ShapeDtypeStruct((P_HEADS, P_NKEYS, T), BF16)
    return pl.pallas_call(
        functools.partial(_route_kernel, tt=tt),
        grid=(T // tt,),
        in_specs=[pl.BlockSpec((D_MODEL, tt), lambda i: (0, i)),
                  pl.BlockSpec((2 * P_HEADS * P_HALF_DIM, D_MODEL), lambda i: (0, 0)),
                  pl.BlockSpec((2 * P_HEADS, P_NKEYS, P_HALF_DIM), lambda i: (0, 0, 0))],
        out_specs=[osp, osp, osp, osp],
        out_shape=[of32, of32, obf, obf],
        scratch_shapes=[pltpu.VMEM((2, 2 * P_HALF_DIM, tt), BF16)],
        compiler_params=_cparams(("arbitrary",)),
        name="peer_route",
    )(h1tb, wpqt, keys)


PEER_EXPERT_BLOCK = 1024
PEER_SUB_BLOCK = 256


def _peer_kernel(h1t_ref, h1tb_ref, e1_ref, lf_ref, e2_ref, r2_ref, pu_ref, pvt_ref, g_ref, b_ref,
                 y_ref, acc_ref, w_ref, *, n_eb):
    eb = pl.program_id(1)
    slot = eb % 2
    rows_per_block = PEER_EXPERT_BLOCK // P_NKEYS
    sub = PEER_SUB_BLOCK
    n_sub = PEER_EXPERT_BLOCK // sub
    tt = acc_ref.shape[1]
    n_parts = 2 if tt % (2 * LANES) == 0 else 1
    pw = tt // n_parts

    def scores(sb):
        return _dot(pu_ref[sb * sub:(sb + 1) * sub, :], h1tb_ref[...])

    def out_part(p):
        return _dot(pvt_ref[...], w_ref[1 - slot, :, p * pw:(p + 1) * pw])

    def gated(a, sb):
        act = (0.5 * a * (1.0 + lax.erf(a * (2.0 ** -0.5)))).astype(BF16)
        gates = []
        for ci in range(sub // P_NKEYS):
            c = eb * rows_per_block + sb * (sub // P_NKEYS) + ci
            gate = jnp.zeros((P_NKEYS, tt), BF16)
            for h in range(P_HEADS):
                lrow = lf_ref[h, pl.ds(c, 1), :].astype(BF16)
                erow = e1_ref[h, pl.ds(c, 1), :].astype(BF16)
                picked = jnp.where(r2_ref[h] < lrow, e2_ref[h], jnp.zeros_like(gate))
                gate = gate + picked * erow
            gates.append(gate)
        w_ref[slot, sb * sub:(sb + 1) * sub, :] = jnp.concatenate(gates, axis=0) * act

    def build(with_out):
        a = [scores(0), scores(1)] + [None] * (n_sub - 2)
        outs = []
        for sb in range(n_sub):
            if with_out and sb < n_parts:
                outs.append(out_part(sb))
            gated(a[sb], sb)
            if sb + 2 < n_sub:
                a[sb + 2] = scores(sb + 2)
        for p, o in enumerate(outs):
            acc_ref[:, p * pw:(p + 1) * pw] += o

    @pl.when(eb == 0)
    def _():
        acc_ref[...] = jnp.zeros(acc_ref.shape, F32)
        build(False)

    @pl.when((eb > 0) & (eb < n_eb))
    def _():
        build(True)

    @pl.when(eb == n_eb)
    def _():
        out = jnp.concatenate([out_part(p) for p in range(n_parts)], axis=1)
        z = (ALPHA * h1t_ref[...] + acc_ref[...] + out).T
        y_ref[...] = _layer_norm_rows(z, g_ref[...], b_ref[...])


def _peer(h1t, h1tb, e1, lf, e2, r2, pu, pvt, ln_g, ln_b):
    T = h1t.shape[1]
    tt = min(T, 512)
    n_eb = P_EXPERTS // PEER_EXPERT_BLOCK
    assert PEER_EXPERT_BLOCK // PEER_SUB_BLOCK >= 2
    tsp = pl.BlockSpec((D_MODEL, tt), lambda i, e: (0, i))
    rsp = pl.BlockSpec((P_HEADS, P_NKEYS, tt), lambda i, e: (0, 0, i))
    vsp = pl.BlockSpec((1, D_MODEL), lambda i, e: (0, 0))
    return pl.pallas_call(
        functools.partial(_peer_kernel, n_eb=n_eb),
        grid=(T // tt, n_eb + 1),
        in_specs=[tsp, tsp, rsp, rsp, rsp, rsp,
                  pl.BlockSpec((PEER_EXPERT_BLOCK, D_MODEL), lambda i, e: (jnp.minimum(e, n_eb - 1), 0)),
                  pl.BlockSpec((D_MODEL, PEER_EXPERT_BLOCK), lambda i, e: (0, jnp.maximum(e - 1, 0))),
                  vsp, vsp],
        out_specs=pl.BlockSpec((tt, D_MODEL), lambda i, e: (i, 0)),
        out_shape=jax.ShapeDtypeStruct((T, D_MODEL), F32),
        scratch_shapes=[pltpu.VMEM((D_MODEL, tt), F32), pltpu.VMEM((2, PEER_EXPERT_BLOCK, tt), BF16)],
        compiler_params=_cparams(("arbitrary", "arbitrary")),
        name="peer_dense",
    )(h1t, h1tb, e1, lf, e2, r2, pu, pvt, ln_g, ln_b)


def _gate_layouts(g, B, S_real, S_pad):
    H, L = M_HEADS, M_CHUNK
    gi = g[:, :H].reshape(B, S_real, H)
    gf = g[:, H:2 * H].reshape(B, S_real, H)
    if S_pad != S_real:
        pw = ((0, 0), (0, S_pad - S_real), (0, 0))
        gi = jnp.pad(gi, pw, constant_values=PAD_LOG_INPUT_GATE)
        gf = jnp.pad(gf, pw, constant_values=-PAD_LOG_INPUT_GATE)
    gi = jnp.swapaxes(gi, 1, 2)
    gf = jnp.swapaxes(gf, 1, 2)
    col = lambda a: a.reshape(B, H, S_pad, 1)
    row = lambda a: a.reshape(B, H, S_pad // L, L)
    return col(gi), col(gf), row(gi), row(gf)


def _pad_seq(a, B, S_real, S_pad):
    if S_pad == S_real:
        return a
    a = a.reshape(B, S_real, a.shape[-1])
    return jnp.pad(a, ((0, 0), (0, S_pad - S_real), (0, 0))).reshape(B * S_pad, a.shape[-1])


def _group(x3, attend, state, W):
    B, S, _ = x3.shape
    T = B * S
    x = x3.reshape(T, D_MODEL)
    q, k, kb, v, u, vm, om, ga, gb, g, vbt = _in_proj(x, W["w_main"], W["b_main"], W["w_gate"],
                                                      W["b_gate"])
    oa = attend(q, k, v, kb, vbt)

    c0, n0, m0, conv0 = state
    S_pad = -(-S // M_CHUNK) * M_CHUNK
    icol, fcol, irow, frow = _gate_layouts(g, B, S, S_pad)
    halo = jnp.pad(conv0, ((0, 0), (CONV_HALO - (M_CONV - 1), 0), (0, 0)))
    hb, c_new, n_new, m_new = _mlstm(
        _pad_seq(u, B, S, S_pad), halo, _pad_seq(vm, B, S, S_pad), _pad_seq(om, B, S, S_pad),
        icol, fcol, irow, frow, W["w_conv"], W["b_conv"], W["w_qm"], W["w_km"], W["w_kmt"], W["mnorm_g"],
        c0, n0.reshape(B, M_HEADS, 1, M_HEAD_DIM),
        jnp.broadcast_to(m0[:, :, None, None], (B, M_HEADS, 1, LANES)), B, S_pad)
    if S_pad != S:
        hb = hb.reshape(B, S_pad, D_MODEL)[:, :S].reshape(T, D_MODEL)
    u3 = u.reshape(B, S, D_MODEL)
    conv_new = jnp.concatenate([conv0, u3], axis=1)[:, S:]

    h1t, h1tb = _merge(x, oa, hb, ga, gb, W["w_a"], W["w_b"], W["w_o"], W["ln1_g"], W["ln1_b"])
    e1, lf, e2, r2 = _route(h1tb, W["w_pqt"], W["p_keys"])
    y = _peer(h1t, h1tb, e1, lf, e2, r2, W["p_u"], W["p_vt"], W["ln2_g"], W["ln2_b"])

    new_state = (k.reshape(1, B, S, A_HEADS, A_HD2), v.reshape(1, B, S, A_HEADS, A_HD2),
                 c_new[None], n_new.reshape(1, B, M_HEADS, M_HEAD_DIM), m_new[None, :, :, 0, 0],
                 conv_new[None])
    return y.reshape(B, S, D_MODEL), new_state


def kernel(x_prompt, x_sample, cache_k, cache_v, state_C, state_n, state_m, state_conv, page_table,
           w_in, b_in, lam_q1, lam_k1, lam_q2, lam_k2, subln_g, w_conv, b_conv, w_qm, w_km, mnorm_g,
           w_a, w_b, w_o, ln1_g, ln1_b, w_pq, p_keys, p_u, p_v, ln2_g, ln2_b):
    assert w_in.shape[0] == DEPTH
    B, S, _ = x_prompt.shape
    DB, DS, _ = x_sample.shape
    assert 2 * DS == SAMPLE_ROWS

    n_pre = 6 * D_MODEL
    wi, bi = w_in[0], b_in[0]
    W = {
        "w_main": jnp.concatenate([wi[:, :n_pre], wi[:, n_pre + GATE_COLS:]], axis=1).astype(BF16),
        "b_main": jnp.concatenate([bi[:n_pre], bi[n_pre + GATE_COLS:]])[None],
        "w_gate": jnp.pad(wi[:, n_pre:n_pre + GATE_COLS], ((0, 0), (0, LANES - GATE_COLS))).astype(BF16),
        "b_gate": jnp.pad(bi[n_pre:n_pre + GATE_COLS], (0, LANES - GATE_COLS))[None],
        "w_conv": w_conv[0], "b_conv": b_conv[0][None],
        "w_qm": w_qm[0].astype(BF16), "w_km": w_km[0].astype(BF16),
        "w_kmt": jnp.swapaxes(w_km[0], 1, 2).astype(BF16),
        "mnorm_g": mnorm_g[0][None],
        "w_a": w_a[0].astype(BF16), "w_b": w_b[0].astype(BF16), "w_o": w_o[0].astype(BF16),
        "ln1_g": ln1_g[0][None], "ln1_b": ln1_b[0][None],
        "w_pqt": w_pq[0].T.astype(BF16),
        "p_keys": p_keys[0].reshape(2 * P_HEADS, P_NKEYS, P_HALF_DIM).astype(BF16),
        "p_u": p_u[0].astype(BF16), "p_vt": p_v[0].T.astype(BF16),
        "ln2_g": ln2_g[0][None], "ln2_b": ln2_b[0][None],
    }
    lam_vecs = (lam_q1[0][None], lam_k1[0][None], lam_q2[0][None], lam_k2[0][None])
    sg = subln_g[0][None]

    def attend_prompt(q, k, v, kb, vbt):
        return _attn_prompt(q, kb, vbt, lam_vecs, subln_g[0][:, None], B, S)

    def attend_sample(q, k, v, kb, vbt):
        q3 = q.reshape(DB, DS, D_MODEL)
        q8 = jnp.concatenate([q3, q3], axis=1)
        padn = ((0, 0), (0, SAMPLE_ROWS - DS), (0, 0), (0, 0))
        kn8 = jnp.pad(k.reshape(DB, DS, A_HEADS, A_HD2), padn)
        vn8 = jnp.pad(v.reshape(DB, DS, A_HEADS, A_HD2), padn)
        o8 = _attn_sample(q8, kn8, vn8, cache_k, cache_v, page_table, lam_vecs, sg)
        return o8[:, :DS].reshape(DB * DS, D_MODEL).astype(BF16)

    zero_state = (jnp.zeros((B, M_HEADS, M_HEAD_DIM, M_HEAD_DIM), F32),
                  jnp.zeros((B, M_HEADS, M_HEAD_DIM), F32),
                  jnp.zeros((B, M_HEADS), F32),
                  jnp.zeros((B, M_CONV - 1, D_MODEL), F32))
    yp, sp = _group(x_prompt, attend_prompt, zero_state, W)
    ys, ss = _group(x_sample, attend_sample,
                    (state_C[0], state_n[0], state_m[0], state_conv[0]), W)
    return (yp, ys) + sp + ss
```
